```python
import jax, jax.numpy as jnp
from jax import lax
import numpy as np

D_MODEL = 1024
BATCH = 8
SEQ = 2048
DEPTH = 2
DEC_BATCH = 128
DEC_SEQ = 1
PAST_LEN = 16384
PAGE_SIZE = 128

N_MIXERS = 2
N_RET_LAYERS = (DEPTH + N_MIXERS - 1) // N_MIXERS
N_LRU_LAYERS = DEPTH // N_MIXERS
RET_HEADS = 4
RET_DK = D_MODEL // RET_HEADS
RET_DV = 2 * D_MODEL // RET_HEADS
RET_CHUNK = 128
ROPE_BASE = 10000.0
LRU_WIDTH = D_MODEL
LRU_BLOCKS = 4
LRU_BW = LRU_WIDTH // LRU_BLOCKS
LRU_CONV = 4
LRU_C = 8.0
MEM_LEN = 256
XA_HEADS = 4
XA_HD = D_MODEL // XA_HEADS
FFN_DIM = 3 * D_MODEL
FFN_CONV = 3
EPS = 1e-6

kernel_name = 'hybrid_retention_rglru_memxattn_convffn_step'

F32 = jnp.float32


def rmsnorm(x, g):
    xf = x.astype(F32)
    y = xf * lax.rsqrt(jnp.mean(xf * xf, axis=-1, keepdims=True) + EPS)
    return (y * g.astype(F32)).astype(x.dtype)


def causal_dwconv(x, buf, w, b):
    width = w.shape[0]
    L = x.shape[1]
    xp = jnp.concatenate([buf.astype(x.dtype), x], axis=1)
    y = b.astype(x.dtype) + xp[:, 0:L] * w[0]
    for t in range(1, width):
        y = y + xp[:, t:t + L] * w[t]
    return y, xp[:, L:]


def rotary(x, offset):
    L, dh = x.shape[1], x.shape[-1]
    half = dh // 2
    inv = ROPE_BASE ** (-jnp.arange(half, dtype=F32) / half)
    pos = offset + jnp.arange(L, dtype=F32)
    ang = pos[:, None] * inv[None, :]
    cos = jnp.cos(ang)[None, :, None, :]
    sin = jnp.sin(ang)[None, :, None, :]
    x1, x2 = x[..., :half], x[..., half:]
    return jnp.concatenate([x1 * cos - x2 * sin, x1 * sin + x2 * cos], axis=-1)


def retention_scan(q, k, v, s0):
    B, L = q.shape[0], q.shape[1]
    C = RET_CHUNK if L % RET_CHUNK == 0 else L
    n = L // C
    log_g = jnp.log(1.0 - 2.0 ** (-5.0 - jnp.arange(RET_HEADS, dtype=F32)))
    idx = jnp.arange(C, dtype=F32)
    rel = idx[:, None] - idx[None, :]
    intra = jnp.where(rel[None] >= 0, jnp.exp(log_g[:, None, None] * jnp.maximum(rel, 0.0)[None]), 0.0)
    q_dec = jnp.exp(log_g[None, :] * (idx[:, None] + 1.0))
    k_dec = jnp.exp(log_g[None, :] * (C - 1.0 - idx[:, None]))
    chunk_dec = jnp.exp(log_g * C)

    def to_chunks(t):
        return t.reshape(B, n, C, *t.shape[2:]).swapaxes(0, 1)

    def step(s, inp):
        qc, kc, vc = inp
        att = jnp.einsum('bihd,bjhd->bhij', qc, kc) * intra
        o = (jnp.einsum('bhij,bjhe->bihe', att, vc)
             + jnp.einsum('bihd,bhde->bihe', qc * q_dec[:, :, None], s))
        s = s * chunk_dec[:, None, None] + jnp.einsum('bjhd,bjhe->bhde', kc * k_dec[:, :, None], vc)
        return s, o

    s, o = lax.scan(step, s0, (to_chunks(q), to_chunks(k), to_chunks(v)))
    return o.swapaxes(0, 1).reshape(B, L, RET_HEADS, RET_DV), s


def retention_mixer(h, s0, offset, w_in, w_out):
    B, L, _ = h.shape
    nqk = RET_HEADS * RET_DK
    nv = RET_HEADS * RET_DV
    q, k, v, g = jnp.split(h @ w_in, [nqk, 2 * nqk, 2 * nqk + nv], axis=-1)
    q = rotary(q.reshape(B, L, RET_HEADS, RET_DK).astype(F32), offset)
    k = rotary(k.reshape(B, L, RET_HEADS, RET_DK).astype(F32), offset) * (RET_DK ** -0.5)
    v = v.reshape(B, L, RET_HEADS, RET_DV).astype(F32)
    o, s = retention_scan(q, k, v, s0.astype(F32))
    o = o * lax.rsqrt(jnp.mean(o * o, axis=-1, keepdims=True) + EPS)
    o = (jax.nn.silu(g.astype(F32)) * o.reshape(B, L, nv)).astype(h.dtype)
    return o @ w_out, s.astype(s0.dtype)


def block_diag(x, w, b):
    B, L, _ = x.shape
    xb = x.reshape(B, L, LRU_BLOCKS, LRU_BW)
    return (jnp.einsum('blhi,hij->blhj', xb, w) + b).reshape(B, L, LRU_WIDTH)


def linear_recurrence(a, u, h0):
    def comb(l, r):
        return (l[0] * r[0], r[0] * l[1] + r[1])
    a_cum, u_cum = lax.associative_scan(comb, (a, u), axis=1)
    return a_cum * h0[:, None, :] + u_cum


def rglru_mixer(h, h0, conv_buf, w_in, conv_w, conv_b, wa, ba, wx, bx, lam, w_out):
    xb, gb = jnp.split(h @ w_in, 2, axis=-1)
    xc, new_buf = causal_dwconv(xb, conv_buf, conv_w, conv_b)
    r = jax.nn.sigmoid(block_diag(xc, wa, ba).astype(F32))
    i = jax.nn.sigmoid(block_diag(xc, wx, bx).astype(F32))
    log_a = LRU_C * r * jax.nn.log_sigmoid(lam.astype(F32))
    a = jnp.exp(log_a)
    u = jnp.sqrt(-jnp.expm1(2.0 * log_a)) * (i * xc.astype(F32))
    hs = linear_recurrence(a, u, h0.astype(F32))
    y = (jax.nn.gelu(gb.astype(F32)) * hs).astype(h.dtype)
    return y @ w_out, hs[:, -1].astype(h0.dtype), new_buf.astype(conv_buf.dtype)


def memory_kv(mem, g, w_kv):
    B, M, _ = mem.shape
    k, v = jnp.split(rmsnorm(mem, g) @ w_kv, 2, axis=-1)
    return k.reshape(B, M, XA_HEADS, XA_HD), v.reshape(B, M, XA_HEADS, XA_HD)


def cross_attn(h, mk, mv, w_q, w_o):
    B, L, _ = h.shape
    q = (h @ w_q).reshape(B, L, XA_HEADS, XA_HD).astype(F32)
    s = jnp.einsum('blhd,bmhd->bhlm', q, mk.astype(F32)) * (XA_HD ** -0.5)
    p = jax.nn.softmax(s, axis=-1)
    o = jnp.einsum('bhlm,bmhd->blhd', p, mv.astype(F32)).reshape(B, L, XA_HEADS * XA_HD)
    return o.astype(h.dtype) @ w_o


def conv_ffn(h, buf, w_up, conv_w, conv_b, w_down):
    u, g = jnp.split(h @ w_up, 2, axis=-1)
    uc, new_buf = causal_dwconv(u, buf, conv_w, conv_b)
    y = (jax.nn.gelu(uc.astype(F32)) * g.astype(F32)).astype(h.dtype)
    return y @ w_down, new_buf.astype(buf.dtype)


def trunk(x, offset, ret_s, lru_h, lru_conv, ffn_conv, mem_k, mem_v, p):
    new_ret, new_h, new_conv, new_ffn = [], [], [], []
    for i in range(DEPTH):
        j = i // N_MIXERS
        hn = rmsnorm(x, p['norm_mix'][i])
        if i % N_MIXERS == 0:
            y, s = retention_mixer(hn, ret_s[j], offset, p['ret_w_in'][j], p['ret_w_out'][j])
            new_ret.append(s)
        else:
            y, hl, cb = rglru_mixer(hn, lru_h[j], lru_conv[j], p['lru_w_in'][j], p['lru_conv_w'][j],
                                    p['lru_conv_b'][j], p['lru_wa'][j], p['lru_ba'][j], p['lru_wx'][j],
                                    p['lru_bx'][j], p['lru_lambda'][j], p['lru_w_out'][j])
            new_h.append(hl)
            new_conv.append(cb)
        x = x + y
        x = x + cross_attn(rmsnorm(x, p['norm_xa'][i]), mem_k[i], mem_v[i], p['xa_w_q'][i], p['xa_w_o'][i])
        y, fb = conv_ffn(rmsnorm(x, p['norm_ffn'][i]), ffn_conv[i], p['ffn_w_up'][i], p['ffn_conv_w'][i],
                         p['ffn_conv_b'][i], p['ffn_w_down'][i])
        x = x + y
        new_ffn.append(fb)
    x = rmsnorm(x, p['norm_final'])
    return x, jnp.stack(new_ret), jnp.stack(new_h), jnp.stack(new_conv), jnp.stack(new_ffn)


def setup_inputs(seed: int = 0) -> dict:
    key = jax.random.key(seed)
    ks = iter(jax.random.split(key, 40))

    def nrm(shape, scale):
        return jax.random.normal(next(ks), shape, F32) * scale

    def gain(shape):
        return 1.0 + nrm(shape, 0.05)

    D = D_MODEL
    a_target = jax.random.uniform(next(ks), (N_LRU_LAYERS, LRU_WIDTH), F32, minval=0.9, maxval=0.999)
    s_base = a_target ** (1.0 / LRU_C)
    lru_lambda = jnp.log(s_base) - jnp.log1p(-s_base)
    ret_cols = 2 * RET_HEADS * RET_DK + 2 * RET_HEADS * RET_DV
    return {
        'x_prompt': nrm((BATCH, SEQ, D), 1.0),
        'x_sample': nrm((DEC_BATCH, DEC_SEQ, D), 1.0),
        'state_ret': nrm((N_RET_LAYERS, DEC_BATCH, RET_HEADS, RET_DK, RET_DV), 0.1),
        'state_lru_h': nrm((N_LRU_LAYERS, DEC_BATCH, LRU_WIDTH), 0.5),
        'state_lru_conv': nrm((N_LRU_LAYERS, DEC_BATCH, LRU_CONV - 1, LRU_WIDTH), 1.0),
        'state_ffn_conv': nrm((DEPTH, DEC_BATCH, FFN_CONV - 1, FFN_DIM), 1.0),
        'cache_mem_k': nrm((DEPTH, DEC_BATCH, MEM_LEN, XA_HEADS, XA_HD), 1.0),
        'cache_mem_v': nrm((DEPTH, DEC_BATCH, MEM_LEN, XA_HEADS, XA_HD), 1.0),
        'mem_prompt': nrm((BATCH, MEM_LEN, D), 1.0),
        'norm_mix': gain((DEPTH, D)),
        'norm_xa': gain((DEPTH, D)),
        'norm_mem': gain((DEPTH, D)),
        'norm_ffn': gain((DEPTH, D)),
        'norm_final': gain((D,)),
        'ret_w_in': nrm((N_RET_LAYERS, D, ret_cols), D ** -0.5),
        'ret_w_out': nrm((N_RET_LAYERS, RET_HEADS * RET_DV, D), (RET_HEADS * RET_DV) ** -0.5),
        'lru_w_in': nrm((N_LRU_LAYERS, D, 2 * LRU_WIDTH), D ** -0.5),
        'lru_conv_w': nrm((N_LRU_LAYERS, LRU_CONV, LRU_WIDTH), LRU_CONV ** -0.5),
        'lru_conv_b': nrm((N_LRU_LAYERS, LRU_WIDTH), 0.01),
        'lru_wa': nrm((N_LRU_LAYERS, LRU_BLOCKS, LRU_BW, LRU_BW), LRU_BW ** -0.5),
        'lru_ba': nrm((N_LRU_LAYERS, LRU_BLOCKS, LRU_BW), 0.01),
        'lru_wx': nrm((N_LRU_LAYERS, LRU_BLOCKS, LRU_BW, LRU_BW), LRU_BW ** -0.5),
        'lru_bx': nrm((N_LRU_LAYERS, LRU_BLOCKS, LRU_BW), 0.01),
        'lru_lambda': lru_lambda,
        'lru_w_out': nrm((N_LRU_LAYERS, LRU_WIDTH, D), LRU_WIDTH ** -0.5),
        'xa_w_q': nrm((DEPTH, D, XA_HEADS * XA_HD), D ** -0.5),
        'xa_w_kv': nrm((DEPTH, D, 2 * XA_HEADS * XA_HD), D ** -0.5),
        'xa_w_o': nrm((DEPTH, XA_HEADS * XA_HD, D), (XA_HEADS * XA_HD) ** -0.5),
        'ffn_w_up': nrm((DEPTH, D, 2 * FFN_DIM), D ** -0.5),
        'ffn_conv_w': nrm((DEPTH, FFN_CONV, FFN_DIM), FFN_CONV ** -0.5),
        'ffn_conv_b': nrm((DEPTH, FFN_DIM), 0.01),
        'ffn_w_down': nrm((DEPTH, FFN_DIM, D), FFN_DIM ** -0.5),
    }


def reference(x_prompt, x_sample, state_ret, state_lru_h, state_lru_conv, state_ffn_conv,
              cache_mem_k, cache_mem_v, mem_prompt, norm_mix, norm_xa, norm_mem, norm_ffn, norm_final,
              ret_w_in, ret_w_out, lru_w_in, lru_conv_w, lru_conv_b, lru_wa, lru_ba, lru_wx, lru_bx,
              lru_lambda, lru_w_out, xa_w_q, xa_w_kv, xa_w_o, ffn_w_up, ffn_conv_w, ffn_conv_b, ffn_w_down):
    p = {
        'norm_mix': norm_mix, 'norm_xa': norm_xa, 'norm_ffn': norm_ffn, 'norm_final': norm_final,
        'ret_w_in': ret_w_in, 'ret_w_out': ret_w_out,
        'lru_w_in': lru_w_in, 'lru_conv_w': lru_conv_w, 'lru_conv_b': lru_conv_b,
        'lru_wa': lru_wa, 'lru_ba': lru_ba, 'lru_wx': lru_wx, 'lru_bx': lru_bx,
        'lru_lambda': lru_lambda, 'lru_w_out': lru_w_out,
        'xa_w_q': xa_w_q, 'xa_w_o': xa_w_o,
        'ffn_w_up': ffn_w_up, 'ffn_conv_w': ffn_conv_w, 'ffn_conv_b': ffn_conv_b, 'ffn_w_down': ffn_w_down,
    }
    bp = x_prompt.shape[0]
    dt = x_prompt.dtype
    mk_list, mv_list = [], []
    for i in range(DEPTH):
        mk_i, mv_i = memory_kv(mem_prompt, norm_mem[i], xa_w_kv[i])
        mk_list.append(mk_i)
        mv_list.append(mv_i)
    new_mem_k_p = jnp.stack(mk_list)
    new_mem_v_p = jnp.stack(mv_list)
    ret0 = jnp.zeros((N_RET_LAYERS, bp, RET_HEADS, RET_DK, RET_DV), dt)
    h0 = jnp.zeros((N_LRU_LAYERS, bp, LRU_WIDTH), dt)
    lconv0 = jnp.zeros((N_LRU_LAYERS, bp, LRU_CONV - 1, LRU_WIDTH), dt)
    fconv0 = jnp.zeros((DEPTH, bp, FFN_CONV - 1, FFN_DIM), dt)
    y_prompt, new_ret_p, new_lru_h_p, new_lru_conv_p, new_ffn_conv_p = trunk(
        x_prompt, 0, ret0, h0, lconv0, fconv0, new_mem_k_p, new_mem_v_p, p)
    y_sample, new_ret_s, new_lru_h_s, new_lru_conv_s, new_ffn_conv_s = trunk(
        x_sample, PAST_LEN, state_ret, state_lru_h, state_lru_conv, state_ffn_conv,
        cache_mem_k, cache_mem_v, p)
    return (y_prompt, y_sample, new_ret_p, new_ret_s, new_lru_h_p, new_lru_h_s,
            new_lru_conv_p, new_lru_conv_s, new_ffn_conv_p, new_ffn_conv_s, new_mem_k_p, new_mem_v_p)
```

```python
import functools
import math

import jax
import jax.numpy as jnp
from jax import lax
from jax.experimental import pallas as pl
from jax.experimental.pallas import tpu as pltpu

F32 = jnp.float32
BF16 = jnp.bfloat16

D_MODEL = 1024
DEPTH = 2
PAST_LEN = 16384
RET_HEADS = 4
RET_DK = D_MODEL // RET_HEADS
RET_DV = 2 * D_MODEL // RET_HEADS
RET_CHUNK = 128
ROPE_BASE = 10000.0
ROPE_HALF = RET_DK // 2
LRU_WIDTH = D_MODEL
LRU_BLOCKS = 4
LRU_BW = LRU_WIDTH // LRU_BLOCKS
LRU_CONV = 4
LRU_C = 8.0
XA_HEADS = 4
XA_HD = D_MODEL // XA_HEADS
FFN_DIM = 3 * D_MODEL
FFN_CONV = 3
EPS = 1e-6

RET_QK = RET_HEADS * RET_DK
RET_V = RET_HEADS * RET_DV
RET_COLS = 2 * RET_QK + 2 * RET_V
RET_GAMMA = tuple(1.0 - 2.0 ** (-5.0 - h) for h in range(RET_HEADS))
RET_LOG_G = tuple(math.log(g) for g in RET_GAMMA)
K_SCALE = RET_DK ** -0.5
XA_SCALE = XA_HD ** -0.5
SQRT_2_OVER_PI = math.sqrt(2.0 / math.pi)

SUBLANES = 8
VMEM_LIMIT = 56 << 20

ROW_TILE = 512
FFN_COLS = 512
STEP_ROWS = 8


def _params(*sem):
    return pltpu.CompilerParams(dimension_semantics=sem, vmem_limit_bytes=VMEM_LIMIT)


def _resident(shape):
    zeros = (0,) * len(shape)
    return pl.BlockSpec(shape, lambda *_: zeros, pipeline_mode=pl.Buffered(1))


def _dot(a, b):
    return jnp.dot(a, b, preferred_element_type=F32)


def _dot_nt(a, b):
    return lax.dot_general(a, b, (((1,), (1,)), ((), ())), preferred_element_type=F32)


def _rms(x, g):
    return x * lax.rsqrt(jnp.mean(x * x, axis=-1, keepdims=True) + EPS) * g


def _gelu(x):
    return x * (0.5 * (1.0 + jnp.tanh(SQRT_2_OVER_PI * (x + 0.044715 * (x * x * x)))))


def _silu(x):
    return x * jax.nn.sigmoid(x)


def _log_sigmoid(x):
    return jnp.minimum(x, 0.0) - jnp.log(1.0 + jnp.exp(-jnp.abs(x)))


def _shift_rows(x, prev, s):
    rows = lax.broadcasted_iota(jnp.int32, (x.shape[0], 1), 0)
    out = pltpu.roll(x, s, 0)
    for t in range(s):
        src = SUBLANES - s + t
        out = jnp.where(rows == t, prev[src:src + 1, :], out)
    return out


def _ret_proj_kernel(x_ref, g_ref, w_ref, cos_ref, sin_ref, o_ref):
    hn = _rms(x_ref[...], g_ref[...]).astype(BF16)
    cos = cos_ref[...]
    sin = sin_ref[...]
    for n in range(2 * RET_HEADS):
        c0 = n * RET_DK
        acc = _dot(hn, w_ref[:, c0:c0 + RET_DK])
        x1 = acc[:, :ROPE_HALF]
        x2 = acc[:, ROPE_HALF:]
        r1 = x1 * cos - x2 * sin
        r2 = x1 * sin + x2 * cos
        if n >= RET_HEADS:
            r1 = r1 * K_SCALE
            r2 = r2 * K_SCALE
        o_ref[:, c0:c0 + ROPE_HALF] = r1.astype(o_ref.dtype)
        o_ref[:, c0 + ROPE_HALF:c0 + RET_DK] = r2.astype(o_ref.dtype)
    for n in range(2 * RET_V // RET_DV):
        c0 = 2 * RET_QK + n * RET_DV
        o_ref[:, c0:c0 + RET_DV] = _dot(hn, w_ref[:, c0:c0 + RET_DV]).astype(o_ref.dtype)


def _ret_proj(x, g, w, cos, sin, tm, out_dtype):
    t = x.shape[0]
    period = cos.shape[0] // tm
    return pl.pallas_call(
        _ret_proj_kernel,
        grid=(t // tm,),
        in_specs=[
            pl.BlockSpec((tm, D_MODEL), lambda m: (m, 0)),
            _resident((1, D_MODEL)),
            _resident((D_MODEL, RET_COLS)),
            pl.BlockSpec((tm, ROPE_HALF), lambda m: (m % period, 0)),
            pl.BlockSpec((tm, ROPE_HALF), lambda m: (m % period, 0)),
        ],
        out_specs=pl.BlockSpec((tm, RET_COLS), lambda m: (m, 0)),
        out_shape=jax.ShapeDtypeStruct((t, RET_COLS), out_dtype),
        compiler_params=_params("arbitrary"),
        name="ret_proj",
    )(x, g, w, cos, sin)


def _ret_scan_kernel(q_ref, k_ref, v_ref, g_ref, x_ref, w_ref, xo_ref, s_ref, y_scr):
    @pl.when(pl.program_id(1) == 0)
    def _():
        s_ref[...] = jnp.zeros_like(s_ref)

    c = RET_CHUNK
    row = lax.broadcasted_iota(jnp.int32, (c, c), 0).astype(F32)
    col = lax.broadcasted_iota(jnp.int32, (c, c), 1).astype(F32)
    rel = row - col
    idx = lax.broadcasted_iota(jnp.int32, (c, 1), 0).astype(F32)
    for h in range(RET_HEADS):
        lg = RET_LOG_G[h]
        intra = jnp.where(rel >= 0, jnp.exp(lg * jnp.maximum(rel, 0.0)), 0.0)
        q_dec = jnp.exp(lg * (idx + 1.0))
        k_dec = jnp.exp(lg * (c - 1.0 - idx))
        chunk_dec = math.exp(lg * c)
        qk_cols = slice(h * RET_DK, (h + 1) * RET_DK)
        v_cols = slice(h * RET_DV, (h + 1) * RET_DV)
        for ci in range(q_ref.shape[0] // c):
            rows = slice(ci * c, (ci + 1) * c)
            qc = q_ref[rows, qk_cols]
            kc = k_ref[rows, qk_cols]
            vc = v_ref[rows, v_cols]
            s = s_ref[h]
            att = _dot_nt(qc, kc) * intra
            o = _dot(att.astype(BF16), vc) + _dot((qc.astype(F32) * q_dec).astype(BF16), s.astype(BF16))
            kd_t = (kc.astype(F32) * k_dec).T.astype(BF16)
            s_ref[h] = s * chunk_dec + _dot(kd_t, vc)
            o = o * lax.rsqrt(jnp.mean(o * o, axis=-1, keepdims=True) + EPS)
            y_scr[rows, v_cols] = (_silu(g_ref[rows, v_cols].astype(F32)) * o).astype(BF16)
    xo_ref[...] = x_ref[...] + _dot(y_scr[...], w_ref[...])


def _ret_scan(qkvg, x, w_out, batch, seq, tm):
    t = x.shape[0]
    nj = seq // tm
    return pl.pallas_call(
        _ret_scan_kernel,
        grid=(batch, nj),
        in_specs=[
            pl.BlockSpec((tm, RET_QK), lambda b, j: (b * nj + j, 0)),
            pl.BlockSpec((tm, RET_QK), lambda b, j: (b * nj + j, 1)),
            pl.BlockSpec((tm, RET_V), lambda b, j: (b * nj + j, 1)),
            pl.BlockSpec((tm, RET_V), lambda b, j: (b * nj + j, 2)),
            pl.BlockSpec((tm, D_MODEL), lambda b, j: (b * nj + j, 0)),
            _resident((RET_V, D_MODEL)),
        ],
        out_specs=[
            pl.BlockSpec((tm, D_MODEL), lambda b, j: (b * nj + j, 0)),
            pl.BlockSpec((None, RET_HEADS, RET_DK, RET_DV), lambda b, j: (b, 0, 0, 0)),
        ],
        out_shape=[
            jax.ShapeDtypeStruct((t, D_MODEL), F32),
            jax.ShapeDtypeStruct((batch, RET_HEADS, RET_DK, RET_DV), F32),
        ],
        scratch_shapes=[pltpu.VMEM((tm, RET_V), BF16)],
        compiler_params=_params("arbitrary", "arbitrary"),
        name="ret_scan",
    )(qkvg, qkvg, qkvg, qkvg, x, w_out)


def _ret_step_kernel(gam_ref, q_ref, k_ref, v_ref, g_ref, s_ref, so_ref, y_ref):
    gamma = gam_ref[pl.program_id(1)]
    q_t = q_ref[...].T
    k_t = k_ref[...].T
    v = v_ref[...]
    gate = _silu(g_ref[...])
    for i in range(STEP_ROWS):
        s_new = s_ref[i, 0] * gamma + k_t[:, i:i + 1] * v[i:i + 1, :]
        so_ref[i, 0] = s_new
        o = jnp.sum(q_t[:, i:i + 1] * s_new, axis=0, keepdims=True)
        o = o * lax.rsqrt(jnp.mean(o * o, axis=-1, keepdims=True) + EPS)
        y_ref[i:i + 1, :] = gate[i:i + 1, :] * o


def _ret_step(qkvg, state):
    t = qkvg.shape[0]
    nk = RET_QK // RET_DK
    nv = RET_V // RET_DV
    state_spec = pl.BlockSpec((None, STEP_ROWS, 1, RET_DK, RET_DV), lambda i, h: (0, i, h, 0, 0))
    return pl.pallas_call(
        _ret_step_kernel,
        grid=(t // STEP_ROWS, RET_HEADS),
        in_specs=[
            pl.BlockSpec(memory_space=pltpu.SMEM),
            pl.BlockSpec((STEP_ROWS, RET_DK), lambda i, h: (i, h)),
            pl.BlockSpec((STEP_ROWS, RET_DK), lambda i, h: (i, nk + h)),
            pl.BlockSpec((STEP_ROWS, RET_DV), lambda i, h: (i, nv + h)),
            pl.BlockSpec((STEP_ROWS, RET_DV), lambda i, h: (i, 2 * nv + h)),
            state_spec,
        ],
        out_specs=[
            state_spec,
            pl.BlockSpec((STEP_ROWS, RET_DV), lambda i, h: (i, h)),
        ],
        out_shape=[
            jax.ShapeDtypeStruct(state.shape, F32),
            jax.ShapeDtypeStruct((t, RET_V), F32),
        ],
        compiler_params=_params("arbitrary", "arbitrary"),
        name="ret_step",
    )(jnp.asarray(RET_GAMMA, F32), qkvg, qkvg, qkvg, qkvg, state)


def _norm_matmul_kernel(x_ref, g_ref, w_ref, o_ref):
    o_ref[...] = _dot(_rms(x_ref[...], g_ref[...]).astype(BF16), w_ref[...])


def _norm_matmul(x, g, w):
    return pl.pallas_call(
        _norm_matmul_kernel,
        out_shape=jax.ShapeDtypeStruct((x.shape[0], w.shape[1]), F32),
        compiler_params=pltpu.CompilerParams(vmem_limit_bytes=VMEM_LIMIT),
        name="norm_matmul",
    )(x, g, w)


def _matmul_res_kernel(y_ref, w_ref, x_ref, o_ref):
    o_ref[...] = x_ref[...] + _dot(y_ref[...].astype(BF16), w_ref[...])


def _matmul_res(y, w, x):
    return pl.pallas_call(
        _matmul_res_kernel,
        out_shape=jax.ShapeDtypeStruct(x.shape, F32),
        compiler_params=pltpu.CompilerParams(vmem_limit_bytes=VMEM_LIMIT),
        name="matmul_res",
    )(y, w, x)


def _mem_kv_kernel(m_ref, g_ref, w_ref, k_ref, v_ref):
    hn = _rms(m_ref[...], g_ref[...]).astype(BF16)
    k_ref[...] = _dot(hn, w_ref[:, :D_MODEL])
    v_ref[...] = _dot(hn, w_ref[:, D_MODEL:])


def _mem_kv(mem, g, w, tm):
    t = mem.shape[0]
    out = jax.ShapeDtypeStruct((DEPTH, t, D_MODEL), F32)
    return pl.pallas_call(
        _mem_kv_kernel,
        grid=(DEPTH, t // tm),
        in_specs=[
            pl.BlockSpec((tm, D_MODEL), lambda i, m: (m, 0)),
            pl.BlockSpec((None, 1, D_MODEL), lambda i, m: (i, 0, 0)),
            pl.BlockSpec((None, D_MODEL, 2 * D_MODEL), lambda i, m: (i, 0, 0)),
        ],
        out_specs=[
            pl.BlockSpec((None, tm, D_MODEL), lambda i, m: (i, m, 0)),
            pl.BlockSpec((None, tm, D_MODEL), lambda i, m: (i, m, 0)),
        ],
        out_shape=[out, out],
        compiler_params=_params("arbitrary", "arbitrary"),
        name="mem_kv",
    )(mem, g, w)


def _xattn_kernel(x_ref, g_ref, wq_ref, mk_ref, mv_ref, wo_ref, xo_ref, o_scr):
    x = x_ref[...]
    q = _dot(_rms(x, g_ref[...]).astype(BF16), wq_ref[...])
    for h in range(XA_HEADS):
        cols = slice(h * XA_HD, (h + 1) * XA_HD)
        s = _dot_nt(q[:, cols].astype(BF16), mk_ref[:, cols].astype(BF16)) * XA_SCALE
        e = jnp.exp(s - jnp.max(s, axis=-1, keepdims=True))
        p = e / jnp.sum(e, axis=-1, keepdims=True)
        o_scr[:, cols] = _dot(p.astype(BF16), mv_ref[:, cols].astype(BF16)).astype(BF16)
    xo_ref[...] = x + _dot(o_scr[...], wo_ref[...])


def _xattn(x, g, wq, mk, mv, wo, layer, batch, seq, tm):
    t = x.shape[0]
    nj = seq // tm
    mem_len = mk.shape[2]
    mem_spec = pl.BlockSpec((None, None, mem_len, D_MODEL), lambda b, j: (layer, b, 0, 0))
    return pl.pallas_call(
        _xattn_kernel,
        grid=(batch, nj),
        in_specs=[
            pl.BlockSpec((tm, D_MODEL), lambda b, j: (b * nj + j, 0)),
            _resident((1, D_MODEL)),
            _resident((D_MODEL, D_MODEL)),
            mem_spec,
            mem_spec,
            _resident((D_MODEL, D_MODEL)),
        ],
        out_specs=pl.BlockSpec((tm, D_MODEL), lambda b, j: (b * nj + j, 0)),
        out_shape=jax.ShapeDtypeStruct((t, D_MODEL), F32),
        scratch_shapes=[pltpu.VMEM((tm, D_MODEL), BF16)],
        compiler_params=_params("arbitrary", "arbitrary"),
        name="xattn",
    )(x, g, wq, mk, mv, wo)


def _xattn_step_kernel(q_ref, mk_ref, mv_ref, o_ref):
    for i in range(STEP_ROWS):
        kq = mk_ref[i] * q_ref[i:i + 1, :]
        for h in range(XA_HEADS):
            cols = slice(h * XA_HD, (h + 1) * XA_HD)
            s = jnp.sum(kq[:, cols], axis=-1, keepdims=True) * XA_SCALE
            e = jnp.exp(s - jnp.max(s, axis=0, keepdims=True))
            p = e / jnp.sum(e, axis=0, keepdims=True)
            o_ref[i:i + 1, cols] = jnp.sum(p * mv_ref[i, :, cols], axis=0, keepdims=True)


def _xattn_step(q, mk, mv, layer):
    t = q.shape[0]
    mem_len = mk.shape[2]
    mem_spec = pl.BlockSpec((None, STEP_ROWS, mem_len, D_MODEL), lambda i: (layer, i, 0, 0))
    return pl.pallas_call(
        _xattn_step_kernel,
        grid=(t // STEP_ROWS,),
        in_specs=[pl.BlockSpec((STEP_ROWS, D_MODEL), lambda i: (i, 0)), mem_spec, mem_spec],
        out_specs=pl.BlockSpec((STEP_ROWS, D_MODEL), lambda i: (i, 0)),
        out_shape=jax.ShapeDtypeStruct((t, D_MODEL), F32),
        compiler_params=_params("arbitrary"),
        name="xattn_step",
    )(q, mk, mv)


def _ffn_tail(x, y_scr, wdn_ref, gf_ref, xo_ref, final):
    out = x + _dot(y_scr[...], wdn_ref[...])
    if final:
        out = _rms(out, gf_ref[...])
    xo_ref[...] = out


def _ffn_kernel(x_ref, g_ref, wup_ref, cw_ref, cb_ref, wdn_ref, gf_ref, xo_ref, buf_ref,
                carry_scr, y_scr, *, final):
    @pl.when(pl.program_id(1) == 0)
    def _():
        carry_scr[...] = jnp.zeros_like(carry_scr)

    x = x_ref[...]
    tm = x.shape[0]
    hn = _rms(x, g_ref[...]).astype(BF16)
    for n in range(FFN_DIM // FFN_COLS):
        cols = slice(n * FFN_COLS, (n + 1) * FFN_COLS)
        gcols = slice(FFN_DIM + n * FFN_COLS, FFN_DIM + (n + 1) * FFN_COLS)
        u = _dot(hn, wup_ref[:, cols])
        gate = _dot(hn, wup_ref[:, gcols])
        prev = carry_scr[:, cols]
        uc = cb_ref[:, cols] + _shift_rows(u, prev, 2) * cw_ref[0:1, cols]
        uc = uc + _shift_rows(u, prev, 1) * cw_ref[1:2, cols]
        uc = uc + u * cw_ref[2:3, cols]
        y_scr[:, cols] = (_gelu(uc) * gate).astype(BF16)
        carry_scr[:, cols] = u[tm - SUBLANES:, :]
        buf_ref[:, cols] = u[tm - (FFN_CONV - 1):, :]
    _ffn_tail(x, y_scr, wdn_ref, gf_ref, xo_ref, final)


def _ffn(x, g, wup, cw, cb, wdn, gf, batch, seq, tm, final):
    t = x.shape[0]
    nj = seq // tm
    return pl.pallas_call(
        functools.partial(_ffn_kernel, final=final),
        grid=(batch, nj),
        in_specs=[
            pl.BlockSpec((tm, D_MODEL), lambda b, j: (b * nj + j, 0)),
            _resident((1, D_MODEL)),
            _resident((D_MODEL, 2 * FFN_DIM)),
            _resident((FFN_CONV, FFN_DIM)),
            _resident((1, FFN_DIM)),
            _resident((FFN_DIM, D_MODEL)),
            _resident((1, D_MODEL)),
        ],
        out_specs=[
            pl.BlockSpec((tm, D_MODEL), lambda b, j: (b * nj + j, 0)),
            pl.BlockSpec((None, FFN_CONV - 1, FFN_DIM), lambda b, j: (b, 0, 0)),
        ],
        out_shape=[
            jax.ShapeDtypeStruct((t, D_MODEL), F32),
            jax.ShapeDtypeStruct((batch, FFN_CONV - 1, FFN_DIM), F32),
        ],
        scratch_shapes=[pltpu.VMEM((SUBLANES, FFN_DIM), F32), pltpu.VMEM((tm, FFN_DIM), BF16)],
        compiler_params=_params("arbitrary", "arbitrary"),
        name="ffn",
    )(x, g, wup, cw, cb, wdn, gf)


def _ffn_step_kernel(x_ref, g_ref, wup_ref, cw_ref, cb_ref, wdn_ref, gf_ref, buf_ref, xo_ref, nbuf_ref,
                     y_scr, *, final):
    x = x_ref[...]
    hn = _rms(x, g_ref[...]).astype(BF16)
    for n in range(FFN_DIM // FFN_COLS):
        cols = slice(n * FFN_COLS, (n + 1) * FFN_COLS)
        gcols = slice(FFN_DIM + n * FFN_COLS, FFN_DIM + (n + 1) * FFN_COLS)
        u = _dot(hn, wup_ref[:, cols])
        gate = _dot(hn, wup_ref[:, gcols])
        b0 = buf_ref[:, cols]
        b1 = buf_ref[:, gcols]
        uc = cb_ref[:, cols] + b0 * cw_ref[0:1, cols]
        uc = uc + b1 * cw_ref[1:2, cols]
        uc = uc + u * cw_ref[2:3, cols]
        y_scr[:, cols] = (_gelu(uc) * gate).astype(BF16)
        nbuf_ref[:, cols] = b1
        nbuf_ref[:, gcols] = u
    _ffn_tail(x, y_scr, wdn_ref, gf_ref, xo_ref, final)


def _ffn_step(x, g, wup, cw, cb, wdn, gf, buf, final):
    t = x.shape[0]
    return pl.pallas_call(
        functools.partial(_ffn_step_kernel, final=final),
        out_shape=[
            jax.ShapeDtypeStruct((t, D_MODEL), F32),
            jax.ShapeDtypeStruct(buf.shape, F32),
        ],
        scratch_shapes=[pltpu.VMEM((t, FFN_DIM), BF16)],
        compiler_params=pltpu.CompilerParams(vmem_limit_bytes=VMEM_LIMIT),
        name="ffn_step",
    )(x, g, wup, cw, cb, wdn, gf, buf)


def _lru_gates(xc, wa_ref, ba_ref, wx_ref, bx_ref, lam_ref, a_out, u_out):
    xcb = xc.astype(BF16)
    log_s = _log_sigmoid(lam_ref[...])
    for blk in range(LRU_BLOCKS):
        cols = slice(blk * LRU_BW, (blk + 1) * LRU_BW)
        r = jax.nn.sigmoid(_dot(xcb[:, cols], wa_ref[blk]) + ba_ref[:, cols])
        i = jax.nn.sigmoid(_dot(xcb[:, cols], wx_ref[blk]) + bx_ref[:, cols])
        log_a = LRU_C * r * log_s[:, cols]
        a = jnp.exp(log_a)
        a_out[:, cols] = a
        u_out[:, cols] = jnp.sqrt(-jnp.tanh(log_a) * (a * a + 1.0)) * (i * xc[:, cols])


def _lru_kernel(x_ref, g_ref, win_ref, cw_ref, cb_ref, wa_ref, ba_ref, wx_ref, bx_ref, lam_ref, wout_ref,
                xo_ref, h_ref, buf_ref, carry_scr, h_scr, a_scr, u_scr):
    @pl.when(pl.program_id(1) == 0)
    def _():
        carry_scr[...] = jnp.zeros_like(carry_scr)
        h_scr[...] = jnp.zeros_like(h_scr)

    x = x_ref[...]
    tm = x.shape[0]
    hn = _rms(x, g_ref[...]).astype(BF16)
    xb = _dot(hn, win_ref[:, :LRU_WIDTH])
    prev = carry_scr[...]
    xc = cb_ref[...] + _shift_rows(xb, prev, 3) * cw_ref[0:1, :]
    xc = xc + _shift_rows(xb, prev, 2) * cw_ref[1:2, :]
    xc = xc + _shift_rows(xb, prev, 1) * cw_ref[2:3, :]
    xc = xc + xb * cw_ref[3:4, :]
    carry_scr[...] = xb[tm - SUBLANES:, :]
    buf_ref[...] = xb[tm - (LRU_CONV - 1):, :]
    _lru_gates(xc, wa_ref, ba_ref, wx_ref, bx_ref, lam_ref, a_scr, u_scr)

    sub = lax.broadcasted_iota(jnp.int32, (SUBLANES, LRU_WIDTH), 0)

    def group(gi, h):
        r0 = pl.multiple_of(gi * SUBLANES, SUBLANES)
        a = a_scr[pl.ds(r0, SUBLANES), :]
        u = u_scr[pl.ds(r0, SUBLANES), :]
        for s in (1, 2, 4):
            keep = sub >= s
            u = jnp.where(keep, a * pltpu.roll(u, s, 0) + u, u)
            a = jnp.where(keep, a * pltpu.roll(a, s, 0), a)
        hs = a * h + u
        u_scr[pl.ds(r0, SUBLANES), :] = hs
        return hs[SUBLANES - 1:, :]

    h_last = lax.fori_loop(0, tm // SUBLANES, group, h_scr[...])
    h_scr[...] = h_last
    h_ref[...] = h_last
    gb = _dot(hn, win_ref[:, LRU_WIDTH:])
    y = (_gelu(gb) * u_scr[...]).astype(BF16)
    xo_ref[...] = x + _dot(y, wout_ref[...])


def _lru_weight_specs():
    return [
        _resident((1, D_MODEL)),
        _resident((D_MODEL, 2 * LRU_WIDTH)),
        _resident((LRU_CONV, LRU_WIDTH)),
        _resident((1, LRU_WIDTH)),
        _resident((LRU_BLOCKS, LRU_BW, LRU_BW)),
        _resident((1, LRU_WIDTH)),
        _resident((LRU_BLOCKS, LRU_BW, LRU_BW)),
        _resident((1, LRU_WIDTH)),
        _resident((1, LRU_WIDTH)),
        _resident((LRU_WIDTH, D_MODEL)),
    ]


def _lru(x, weights, batch, seq, tm):
    t = x.shape[0]
    nj = seq // tm
    return pl.pallas_call(
        _lru_kernel,
        grid=(batch, nj),
        in_specs=[pl.BlockSpec((tm, D_MODEL), lambda b, j: (b * nj + j, 0))] + _lru_weight_specs(),
        out_specs=[
            pl.BlockSpec((tm, D_MODEL), lambda b, j: (b * nj + j, 0)),
            pl.BlockSpec((None, 1, LRU_WIDTH), lambda b, j: (b, 0, 0)),
            pl.BlockSpec((None, LRU_CONV - 1, LRU_WIDTH), lambda b, j: (b, 0, 0)),
        ],
        out_shape=[
            jax.ShapeDtypeStruct((t, D_MODEL), F32),
            jax.ShapeDtypeStruct((batch, 1, LRU_WIDTH), F32),
            jax.ShapeDtypeStruct((batch, LRU_CONV - 1, LRU_WIDTH), F32),
        ],
        scratch_shapes=[
            pltpu.VMEM((SUBLANES, LRU_WIDTH), F32),
            pltpu.VMEM((1, LRU_WIDTH), F32),
            pltpu.VMEM((tm, LRU_WIDTH), F32),
            pltpu.VMEM((tm, LRU_WIDTH), F32),
        ],
        compiler_params=_params("arbitrary", "arbitrary"),
        name="lru",
    )(x, *weights)


def _lru_step_kernel(x_ref, g_ref, win_ref, cw_ref, cb_ref, wa_ref, ba_ref, wx_ref, bx_ref, lam_ref, wout_ref,
                     h0_ref, buf_ref, xo_ref, h_ref, nbuf_ref, a_scr, u_scr):
    x = x_ref[...]
    w = LRU_WIDTH
    hn = _rms(x, g_ref[...]).astype(BF16)
    xb = _dot(hn, win_ref[:, :w])
    xc = cb_ref[...] + buf_ref[:, 0:w] * cw_ref[0:1, :]
    xc = xc + buf_ref[:, w:2 * w] * cw_ref[1:2, :]
    xc = xc + buf_ref[:, 2 * w:3 * w] * cw_ref[2:3, :]
    xc = xc + xb * cw_ref[3:4, :]
    nbuf_ref[:, 0:2 * w] = buf_ref[:, w:3 * w]
    nbuf_ref[:, 2 * w:3 * w] = xb
    _lru_gates(xc, wa_ref, ba_ref, wx_ref, bx_ref, lam_ref, a_scr, u_scr)
    hs = a_scr[...] * h0_ref[...] + u_scr[...]
    h_ref[...] = hs
    gb = _dot(hn, win_ref[:, w:])
    xo_ref[...] = x + _dot((_gelu(gb) * hs).astype(BF16), wout_ref[...])


def _lru_step(x, weights, h0, buf):
    t = x.shape[0]
    return pl.pallas_call(
        _lru_step_kernel,
        out_shape=[
            jax.ShapeDtypeStruct((t, D_MODEL), F32),
            jax.ShapeDtypeStruct(h0.shape, F32),
            jax.ShapeDtypeStruct(buf.shape, F32),
        ],
        scratch_shapes=[pltpu.VMEM((t, LRU_WIDTH), F32), pltpu.VMEM((t, LRU_WIDTH), F32)],
        compiler_params=pltpu.CompilerParams(vmem_limit_bytes=VMEM_LIMIT),
        name="lru_step",
    )(x, *weights, h0, buf)


def _rope_tables(positions):
    inv = ROPE_BASE ** (-jnp.arange(ROPE_HALF, dtype=F32) / ROPE_HALF)
    ang = positions[:, None] * inv[None, :]
    return jnp.cos(ang), jnp.sin(ang)


def kernel(x_prompt, x_sample, state_ret, state_lru_h, state_lru_conv, state_ffn_conv, cache_mem_k, cache_mem_v, mem_prompt, norm_mix, norm_xa, norm_mem, norm_ffn, norm_final, ret_w_in, ret_w_out, lru_w_in, lru_conv_w, lru_conv_b, lru_wa, lru_ba, lru_wx, lru_bx, lru_lambda, lru_w_out, xa_w_q, xa_w_kv, xa_w_o, ffn_w_up, ffn_conv_w, ffn_conv_b, ffn_w_down):
    bp, seq, d = x_prompt.shape
    bs = x_sample.shape[0]
    mem_len = mem_prompt.shape[1]
    assert d == D_MODEL and x_sample.shape[1] == 1
    assert seq % ROW_TILE == 0 and ROW_TILE % RET_CHUNK == 0 and bs % STEP_ROWS == 0

    row = lambda v: v.reshape(1, -1)
    ret_w_in_b = ret_w_in[0].astype(BF16)
    ret_w_out_b = ret_w_out[0].astype(BF16)
    xa_w_q_b = xa_w_q.astype(BF16)
    xa_w_kv_b = xa_w_kv.astype(BF16)
    xa_w_o_b = xa_w_o.astype(BF16)
    ffn_w_up_b = ffn_w_up.astype(BF16)
    ffn_w_down_b = ffn_w_down.astype(BF16)
    lru_weights = lambda j: (
        row(norm_mix[1]), lru_w_in[j].astype(BF16), lru_conv_w[j], row(lru_conv_b[j]),
        lru_wa[j].astype(BF16), row(lru_ba[j]), lru_wx[j].astype(BF16), row(lru_bx[j]),
        row(lru_lambda[j]), lru_w_out[j].astype(BF16))
    gf = row(norm_final)

    mem_k, mem_v = _mem_kv(mem_prompt.reshape(bp * mem_len, d), norm_mem.reshape(DEPTH, 1, d), xa_w_kv_b, ROW_TILE)
    mem_k = mem_k.reshape(DEPTH, bp, mem_len, d)
    mem_v = mem_v.reshape(DEPTH, bp, mem_len, d)
    cos_p, sin_p = _rope_tables(jnp.arange(seq, dtype=F32))
    xp = x_prompt.reshape(bp * seq, d)
    qkvg = _ret_proj(xp, row(norm_mix[0]), ret_w_in_b, cos_p, sin_p, ROW_TILE, BF16)
    xp, ret_p = _ret_scan(qkvg, xp, ret_w_out_b, bp, seq, ROW_TILE)
    xp = _xattn(xp, row(norm_xa[0]), xa_w_q_b[0], mem_k, mem_v, xa_w_o_b[0], 0, bp, seq, ROW_TILE)
    xp, ffn_p0 = _ffn(xp, row(norm_ffn[0]), ffn_w_up_b[0], ffn_conv_w[0], row(ffn_conv_b[0]), ffn_w_down_b[0],
                      gf, bp, seq, ROW_TILE, False)
    xp, lru_h_p, lru_conv_p = _lru(xp, lru_weights(0), bp, seq, ROW_TILE)
    xp = _xattn(xp, row(norm_xa[1]), xa_w_q_b[1], mem_k, mem_v, xa_w_o_b[1], 1, bp, seq, ROW_TILE)
    xp, ffn_p1 = _ffn(xp, row(norm_ffn[1]), ffn_w_up_b[1], ffn_conv_w[1], row(ffn_conv_b[1]), ffn_w_down_b[1],
                      gf, bp, seq, ROW_TILE, True)

    cos_s, sin_s = _rope_tables(jnp.full((bs,), PAST_LEN, F32))
    cache_k = cache_mem_k.reshape(DEPTH, bs, mem_len, d)
    cache_v = cache_mem_v.reshape(DEPTH, bs, mem_len, d)
    xs = x_sample.reshape(bs, d)
    qkvg_s = _ret_proj(xs, row(norm_mix[0]), ret_w_in_b, cos_s, sin_s, bs, F32)
    ret_s, y_s = _ret_step(qkvg_s, state_ret)
    xs = _matmul_res(y_s, ret_w_out_b, xs)
    q_s = _norm_matmul(xs, row(norm_xa[0]), xa_w_q_b[0])
    xs = _matmul_res(_xattn_step(q_s, cache_k, cache_v, 0), xa_w_o_b[0], xs)
    xs, ffn_s0 = _ffn_step(xs, row(norm_ffn[0]), ffn_w_up_b[0], ffn_conv_w[0], row(ffn_conv_b[0]), ffn_w_down_b[0],
                           gf, state_ffn_conv[0].reshape(bs, -1), False)
    xs, lru_h_s, lru_conv_s = _lru_step(xs, lru_weights(0), state_lru_h[0], state_lru_conv[0].reshape(bs, -1))
    q_s = _norm_matmul(xs, row(norm_xa[1]), xa_w_q_b[1])
    xs = _matmul_res(_xattn_step(q_s, cache_k, cache_v, 1), xa_w_o_b[1], xs)
    xs, ffn_s1 = _ffn_step(xs, row(norm_ffn[1]), ffn_w_up_b[1], ffn_conv_w[1], row(ffn_conv_b[1]), ffn_w_down_b[1],
                           gf, state_ffn_conv[1].reshape(bs, -1), True)

    mem_shape = (DEPTH, bp, mem_len, XA_HEADS, XA_HD)
    return (
        xp.reshape(bp, seq, d),
        xs.reshape(bs, 1, d),
        ret_p[None],
        ret_s,
        lru_h_p.reshape(1, bp, LRU_WIDTH),
        lru_h_s[None],
        lru_conv_p[None],
        lru_conv_s.reshape(1, bs, LRU_CONV - 1, LRU_WIDTH),
        jnp.stack([ffn_p0, ffn_p1]),
        jnp.stack([ffn_s0, ffn_s1]).reshape(DEPTH, bs, FFN_CONV - 1, FFN_DIM),
        mem_k.reshape(mem_shape),
        mem_v.reshape(mem_shape),
    )
```

```python
import functools
import math

import jax
import jax.numpy as jnp
from jax import lax
from jax.experimental import pallas as pl
from jax.experimental.pallas import tpu as pltpu

F32 = jnp.float32
BF16 = jnp.bfloat16

D_MODEL = 1024
DEPTH = 2
PAST_LEN = 16384
RET_HEADS = 4
RET_DK = D_MODEL // RET_HEADS
RET_DV = 2 * D_MODEL // RET_HEADS
ROPE_BASE = 10000.0
ROPE_HALF = RET_DK // 2
LRU_WIDTH = D_MODEL
LRU_BLOCKS = 4
LRU_BW = LRU_WIDTH // LRU_BLOCKS
LRU_CONV = 4
LRU_C = 8.0
XA_HEADS = 4
XA_HD = D_MODEL // XA_HEADS
FFN_DIM = 3 * D_MODEL
FFN_CONV = 3
EPS = 1e-6

RET_QK = RET_HEADS * RET_DK
RET_V = RET_HEADS * RET_DV
RET_COLS = 2 * RET_QK + 2 * RET_V
RET_GAMMA = tuple(1.0 - 2.0 ** (-5.0 - h) for h in range(RET_HEADS))
RET_LOG_G = tuple(math.log(g) for g in RET_GAMMA)
K_SCALE = RET_DK ** -0.5
XA_SCALE = XA_HD ** -0.5
SQRT_2_OVER_PI = math.sqrt(2.0 / math.pi)

SUBLANES = 8
VMEM_LIMIT = 56 << 20

ROW_TILE = 512
SCAN_CHUNK = 256
FFN_COLS = 512
STEP_ROWS = 8


def _params(*sem):
    return pltpu.CompilerParams(dimension_semantics=sem, vmem_limit_bytes=VMEM_LIMIT)


def _resident(shape):
    zeros = (0,) * len(shape)
    return pl.BlockSpec(shape, lambda *_: zeros, pipeline_mode=pl.Buffered(1))


def _dot(a, b):
    return jnp.dot(a, b, preferred_element_type=F32)


def _dot_nt(a, b):
    return lax.dot_general(a, b, (((1,), (1,)), ((), ())), preferred_element_type=F32)


def _rms(x, g):
    return x * lax.rsqrt(jnp.mean(x * x, axis=-1, keepdims=True) + EPS) * g


def _gelu(x):
    return x * (0.5 * (1.0 + jnp.tanh(SQRT_2_OVER_PI * (x + 0.044715 * (x * x * x)))))


def _silu(x):
    return x * jax.nn.sigmoid(x)


def _log_sigmoid(x):
    return jnp.minimum(x, 0.0) - jnp.log(1.0 + jnp.exp(-jnp.abs(x)))


def _shift_rows(x, prev, s):
    rolled = pltpu.roll(x, s, 0)
    rows = lax.broadcasted_iota(jnp.int32, (SUBLANES, 1), 0)
    head = jnp.where(rows < s, pltpu.roll(prev, s, 0), rolled[:SUBLANES])
    return jnp.concatenate([head, rolled[SUBLANES:]], axis=0)


def _ret_proj_kernel(x_ref, g_ref, w_ref, cos_ref, sin_ref, o_ref):
    hn = _rms(x_ref[...], g_ref[...]).astype(BF16)
    cos = cos_ref[...]
    sin = sin_ref[...]
    for n in range(2 * RET_HEADS):
        c0 = n * RET_DK
        acc = _dot(hn, w_ref[:, c0:c0 + RET_DK])
        x1 = acc[:, :ROPE_HALF]
        x2 = acc[:, ROPE_HALF:]
        r1 = x1 * cos - x2 * sin
        r2 = x1 * sin + x2 * cos
        if n >= RET_HEADS:
            r1 = r1 * K_SCALE
            r2 = r2 * K_SCALE
        o_ref[:, c0:c0 + ROPE_HALF] = r1.astype(o_ref.dtype)
        o_ref[:, c0 + ROPE_HALF:c0 + RET_DK] = r2.astype(o_ref.dtype)
    for n in range(2 * RET_V // RET_DV):
        c0 = 2 * RET_QK + n * RET_DV
        o_ref[:, c0:c0 + RET_DV] = _dot(hn, w_ref[:, c0:c0 + RET_DV]).astype(o_ref.dtype)


def _ret_proj(x, g, w, cos, sin, tm, out_dtype):
    t = x.shape[0]
    period = cos.shape[0] // tm
    return pl.pallas_call(
        _ret_proj_kernel,
        grid=(t // tm,),
        in_specs=[
            pl.BlockSpec((tm, D_MODEL), lambda m: (m, 0)),
            _resident((1, D_MODEL)),
            _resident((D_MODEL, RET_COLS)),
            pl.BlockSpec((tm, ROPE_HALF), lambda m: (m % period, 0)),
            pl.BlockSpec((tm, ROPE_HALF), lambda m: (m % period, 0)),
        ],
        out_specs=pl.BlockSpec((tm, RET_COLS), lambda m: (m, 0)),
        out_shape=jax.ShapeDtypeStruct((t, RET_COLS), out_dtype),
        compiler_params=_params("arbitrary"),
        name="ret_proj",
    )(x, g, w, cos, sin)


def _ret_scan_kernel(q_ref, k_ref, v_ref, g_ref, x_ref, w_ref, xo_ref, s_ref, y_scr, intra_scr):
    c = SCAN_CHUNK

    @pl.when((pl.program_id(0) == 0) & (pl.program_id(1) == 0))
    def _():
        row = lax.broadcasted_iota(jnp.int32, (c, c), 0).astype(F32)
        col = lax.broadcasted_iota(jnp.int32, (c, c), 1).astype(F32)
        rel = row - col
        for h in range(RET_HEADS):
            intra_scr[h] = jnp.where(rel >= 0, jnp.exp(RET_LOG_G[h] * jnp.maximum(rel, 0.0)), 0.0)

    @pl.when(pl.program_id(1) == 0)
    def _():
        s_ref[...] = jnp.zeros_like(s_ref)

    idx = lax.broadcasted_iota(jnp.int32, (c, 1), 0).astype(F32)
    for h in range(RET_HEADS):
        lg = RET_LOG_G[h]
        intra = intra_scr[h]
        q_dec = jnp.exp(lg * (idx + 1.0))
        k_dec = jnp.exp(lg * (c - 1.0 - idx))
        chunk_dec = math.exp(lg * c)
        qk_cols = slice(h * RET_DK, (h + 1) * RET_DK)
        v_cols = slice(h * RET_DV, (h + 1) * RET_DV)
        for ci in range(q_ref.shape[0] // c):
            rows = slice(ci * c, (ci + 1) * c)
            qc = q_ref[rows, qk_cols]
            kc = k_ref[rows, qk_cols]
            vc = v_ref[rows, v_cols]
            s = s_ref[h]
            att = _dot_nt(qc, kc) * intra
            o = _dot(att.astype(BF16), vc) + _dot((qc.astype(F32) * q_dec).astype(BF16), s.astype(BF16))
            kd_t = (kc.astype(F32) * k_dec).T.astype(BF16)
            s_ref[h] = s * chunk_dec + _dot(kd_t, vc)
            o = o * lax.rsqrt(jnp.mean(o * o, axis=-1, keepdims=True) + EPS)
            y_scr[rows, v_cols] = (_silu(g_ref[rows, v_cols].astype(F32)) * o).astype(BF16)
    xo_ref[...] = x_ref[...] + _dot(y_scr[...], w_ref[...])


def _ret_scan(qkvg, x, w_out, batch, seq, tm):
    t = x.shape[0]
    nj = seq // tm
    return pl.pallas_call(
        _ret_scan_kernel,
        grid=(batch, nj),
        in_specs=[
            pl.BlockSpec((tm, RET_QK), lambda b, j: (b * nj + j, 0)),
            pl.BlockSpec((tm, RET_QK), lambda b, j: (b * nj + j, 1)),
            pl.BlockSpec((tm, RET_V), lambda b, j: (b * nj + j, 1)),
            pl.BlockSpec((tm, RET_V), lambda b, j: (b * nj + j, 2)),
            pl.BlockSpec((tm, D_MODEL), lambda b, j: (b * nj + j, 0)),
            _resident((RET_V, D_MODEL)),
        ],
        out_specs=[
            pl.BlockSpec((tm, D_MODEL), lambda b, j: (b * nj + j, 0)),
            pl.BlockSpec((None, RET_HEADS, RET_DK, RET_DV), lambda b, j: (b, 0, 0, 0)),
        ],
        out_shape=[
            jax.ShapeDtypeStruct((t, D_MODEL), F32),
            jax.ShapeDtypeStruct((batch, RET_HEADS, RET_DK, RET_DV), F32),
        ],
        scratch_shapes=[
            pltpu.VMEM((tm, RET_V), BF16),
            pltpu.VMEM((RET_HEADS, SCAN_CHUNK, SCAN_CHUNK), F32),
        ],
        compiler_params=_params("arbitrary", "arbitrary"),
        name="ret_scan",
    )(qkvg, qkvg, qkvg, qkvg, x, w_out)


def _ret_step_kernel(gam_ref, q_ref, k_ref, v_ref, g_ref, s_ref, so_ref, y_ref):
    gamma = gam_ref[pl.program_id(1)]
    q_t = q_ref[...].T
    k_t = k_ref[...].T
    v = v_ref[...]
    gate = _silu(g_ref[...])
    for i in range(STEP_ROWS):
        s_new = s_ref[i, 0] * gamma + k_t[:, i:i + 1] * v[i:i + 1, :]
        so_ref[i, 0] = s_new
        o = jnp.sum(q_t[:, i:i + 1] * s_new, axis=0, keepdims=True)
        o = o * lax.rsqrt(jnp.mean(o * o, axis=-1, keepdims=True) + EPS)
        y_ref[i:i + 1, :] = gate[i:i + 1, :] * o


def _ret_step(qkvg, state):
    t = qkvg.shape[0]
    nk = RET_QK // RET_DK
    nv = RET_V // RET_DV
    state_spec = pl.BlockSpec((None, STEP_ROWS, 1, RET_DK, RET_DV), lambda i, h: (0, i, h, 0, 0))
    return pl.pallas_call(
        _ret_step_kernel,
        grid=(t // STEP_ROWS, RET_HEADS),
        in_specs=[
            pl.BlockSpec(memory_space=pltpu.SMEM),
            pl.BlockSpec((STEP_ROWS, RET_DK), lambda i, h: (i, h)),
            pl.BlockSpec((STEP_ROWS, RET_DK), lambda i, h: (i, nk + h)),
            pl.BlockSpec((STEP_ROWS, RET_DV), lambda i, h: (i, nv + h)),
            pl.BlockSpec((STEP_ROWS, RET_DV), lambda i, h: (i, 2 * nv + h)),
            state_spec,
        ],
        out_specs=[
            state_spec,
            pl.BlockSpec((STEP_ROWS, RET_DV), lambda i, h: (i, h)),
        ],
        out_shape=[
            jax.ShapeDtypeStruct(state.shape, F32),
            jax.ShapeDtypeStruct((t, RET_V), F32),
        ],
        compiler_params=_params("arbitrary", "arbitrary"),
        name="ret_step",
    )(jnp.asarray(RET_GAMMA, F32), qkvg, qkvg, qkvg, qkvg, state)


def _norm_matmul_kernel(x_ref, g_ref, w_ref, o_ref):
    o_ref[...] = _dot(_rms(x_ref[...], g_ref[...]).astype(BF16), w_ref[...])


def _norm_matmul(x, g, w):
    return pl.pallas_call(
        _norm_matmul_kernel,
        out_shape=jax.ShapeDtypeStruct((x.shape[0], w.shape[1]), F32),
        compiler_params=pltpu.CompilerParams(vmem_limit_bytes=VMEM_LIMIT),
        name="norm_matmul",
    )(x, g, w)


def _matmul_res_kernel(y_ref, w_ref, x_ref, o_ref):
    o_ref[...] = x_ref[...] + _dot(y_ref[...].astype(BF16), w_ref[...])


def _matmul_res(y, w, x):
    return pl.pallas_call(
        _matmul_res_kernel,
        out_shape=jax.ShapeDtypeStruct(x.shape, F32),
        compiler_params=pltpu.CompilerParams(vmem_limit_bytes=VMEM_LIMIT),
        name="matmul_res",
    )(y, w, x)


def _mem_kv_kernel(m_ref, g_ref, w_ref, k_ref, v_ref):
    hn = _rms(m_ref[...], g_ref[...]).astype(BF16)
    k_ref[...] = _dot(hn, w_ref[:, :D_MODEL])
    v_ref[...] = _dot(hn, w_ref[:, D_MODEL:])


def _mem_kv(mem, g, w, tm):
    t = mem.shape[0]
    out = jax.ShapeDtypeStruct((DEPTH, t, D_MODEL), F32)
    return pl.pallas_call(
        _mem_kv_kernel,
        grid=(DEPTH, t // tm),
        in_specs=[
            pl.BlockSpec((tm, D_MODEL), lambda i, m: (m, 0)),
            pl.BlockSpec((None, 1, D_MODEL), lambda i, m: (i, 0, 0)),
            pl.BlockSpec((None, D_MODEL, 2 * D_MODEL), lambda i, m: (i, 0, 0)),
        ],
        out_specs=[
            pl.BlockSpec((None, tm, D_MODEL), lambda i, m: (i, m, 0)),
            pl.BlockSpec((None, tm, D_MODEL), lambda i, m: (i, m, 0)),
        ],
        out_shape=[out, out],
        compiler_params=_params("arbitrary", "arbitrary"),
        name="mem_kv",
    )(mem, g, w)


def _xattn_kernel(x_ref, g_ref, wq_ref, mk_ref, mv_ref, wo_ref, xo_ref, o_scr):
    x = x_ref[...]
    q = _dot(_rms(x, g_ref[...]).astype(BF16), wq_ref[...])
    for h in range(XA_HEADS):
        cols = slice(h * XA_HD, (h + 1) * XA_HD)
        s = _dot_nt(q[:, cols].astype(BF16), mk_ref[:, cols].astype(BF16)) * XA_SCALE
        e = jnp.exp(s - jnp.max(s, axis=-1, keepdims=True))
        p = e / jnp.sum(e, axis=-1, keepdims=True)
        o_scr[:, cols] = _dot(p.astype(BF16), mv_ref[:, cols].astype(BF16)).astype(BF16)
    xo_ref[...] = x + _dot(o_scr[...], wo_ref[...])


def _xattn(x, g, wq, mk, mv, wo, layer, batch, seq, tm):
    t = x.shape[0]
    nj = seq // tm
    mem_len = mk.shape[2]
    mem_spec = pl.BlockSpec((None, None, mem_len, D_MODEL), lambda b, j: (layer, b, 0, 0))
    return pl.pallas_call(
        _xattn_kernel,
        grid=(batch, nj),
        in_specs=[
            pl.BlockSpec((tm, D_MODEL), lambda b, j: (b * nj + j, 0)),
            _resident((1, D_MODEL)),
            _resident((D_MODEL, D_MODEL)),
            mem_spec,
            mem_spec,
            _resident((D_MODEL, D_MODEL)),
        ],
        out_specs=pl.BlockSpec((tm, D_MODEL), lambda b, j: (b * nj + j, 0)),
        out_shape=jax.ShapeDtypeStruct((t, D_MODEL), F32),
        scratch_shapes=[pltpu.VMEM((tm, D_MODEL), BF16)],
        compiler_params=_params("arbitrary", "arbitrary"),
        name="xattn",
    )(x, g, wq, mk, mv, wo)


def _xattn_step_kernel(q_ref, mk_ref, mv_ref, o_ref):
    for i in range(STEP_ROWS):
        s = jnp.sum(mk_ref[i] * q_ref[i][None], axis=-1, keepdims=True) * XA_SCALE
        e = jnp.exp(s - jnp.max(s, axis=0, keepdims=True))
        p = e / jnp.sum(e, axis=0, keepdims=True)
        o_ref[i] = jnp.sum(p * mv_ref[i], axis=0)


def _xattn_step(q, mk, mv, layer):
    t = q.shape[0]
    mem_len = mk.shape[2]
    row_spec = pl.BlockSpec((STEP_ROWS, XA_HEADS, XA_HD), lambda i: (i, 0, 0))
    mem_spec = pl.BlockSpec((None, STEP_ROWS, mem_len, XA_HEADS, XA_HD), lambda i: (layer, i, 0, 0, 0))
    return pl.pallas_call(
        _xattn_step_kernel,
        grid=(t // STEP_ROWS,),
        in_specs=[row_spec, mem_spec, mem_spec],
        out_specs=row_spec,
        out_shape=jax.ShapeDtypeStruct((t, XA_HEADS, XA_HD), F32),
        compiler_params=_params("arbitrary"),
        name="xattn_step",
    )(q, mk, mv)


def _ffn_tail(x, y_scr, wdn_ref, gf_ref, xo_ref, final):
    out = x + _dot(y_scr[...], wdn_ref[...])
    if final:
        out = _rms(out, gf_ref[...])
    xo_ref[...] = out


def _ffn_kernel(x_ref, g_ref, wup_ref, cw_ref, cb_ref, wdn_ref, gf_ref, xo_ref, buf_ref,
                carry_scr, y_scr, *, final):
    @pl.when(pl.program_id(1) == 0)
    def _():
        carry_scr[...] = jnp.zeros_like(carry_scr)

    x = x_ref[...]
    tm = x.shape[0]
    hn = _rms(x, g_ref[...]).astype(BF16)
    for n in range(FFN_DIM // FFN_COLS):
        cols = slice(n * FFN_COLS, (n + 1) * FFN_COLS)
        gcols = slice(FFN_DIM + n * FFN_COLS, FFN_DIM + (n + 1) * FFN_COLS)
        u = _dot(hn, wup_ref[:, cols])
        gate = _dot(hn, wup_ref[:, gcols])
        prev = carry_scr[:, cols]
        uc = cb_ref[:, cols] + _shift_rows(u, prev, 2) * cw_ref[0:1, cols]
        uc = uc + _shift_rows(u, prev, 1) * cw_ref[1:2, cols]
        uc = uc + u * cw_ref[2:3, cols]
        y_scr[:, cols] = (_gelu(uc) * gate).astype(BF16)
        carry_scr[:, cols] = u[tm - SUBLANES:, :]
        buf_ref[:, cols] = u[tm - (FFN_CONV - 1):, :]
    _ffn_tail(x, y_scr, wdn_ref, gf_ref, xo_ref, final)


def _ffn(x, g, wup, cw, cb, wdn, gf, batch, seq, tm, final):
    t = x.shape[0]
    nj = seq // tm
    return pl.pallas_call(
        functools.partial(_ffn_kernel, final=final),
        grid=(batch, nj),
        in_specs=[
            pl.BlockSpec((tm, D_MODEL), lambda b, j: (b * nj + j, 0)),
            _resident((1, D_MODEL)),
            _resident((D_MODEL, 2 * FFN_DIM)),
            _resident((FFN_CONV, FFN_DIM)),
            _resident((1, FFN_DIM)),
            _resident((FFN_DIM, D_MODEL)),
            _resident((1, D_MODEL)),
        ],
        out_specs=[
            pl.BlockSpec((tm, D_MODEL), lambda b, j: (b * nj + j, 0)),
            pl.BlockSpec((None, FFN_CONV - 1, FFN_DIM), lambda b, j: (b, 0, 0)),
        ],
        out_shape=[
            jax.ShapeDtypeStruct((t, D_MODEL), F32),
            jax.ShapeDtypeStruct((batch, FFN_CONV - 1, FFN_DIM), F32),
        ],
        scratch_shapes=[pltpu.VMEM((SUBLANES, FFN_DIM), F32), pltpu.VMEM((tm, FFN_DIM), BF16)],
        compiler_params=_params("arbitrary", "arbitrary"),
        name="ffn",
    )(x, g, wup, cw, cb, wdn, gf)


def _ffn_step_kernel(x_ref, g_ref, wup_ref, cw_ref, cb_ref, wdn_ref, gf_ref, buf_ref, xo_ref, nbuf_ref,
                     y_scr, *, final):
    x = x_ref[...]
    hn = _rms(x, g_ref[...]).astype(BF16)
    for n in range(FFN_DIM // FFN_COLS):
        cols = slice(n * FFN_COLS, (n + 1) * FFN_COLS)
        gcols = slice(FFN_DIM + n * FFN_COLS, FFN_DIM + (n + 1) * FFN_COLS)
        u = _dot(hn, wup_ref[:, cols])
        gate = _dot(hn, wup_ref[:, gcols])
        b0 = buf_ref[:, cols]
        b1 = buf_ref[:, gcols]
        uc = cb_ref[:, cols] + b0 * cw_ref[0:1, cols]
        uc = uc + b1 * cw_ref[1:2, cols]
        uc = uc + u * cw_ref[2:3, cols]
        y_scr[:, cols] = (_gelu(uc) * gate).astype(BF16)
        nbuf_ref[:, cols] = b1
        nbuf_ref[:, gcols] = u
    _ffn_tail(x, y_scr, wdn_ref, gf_ref, xo_ref, final)


def _ffn_step(x, g, wup, cw, cb, wdn, gf, buf, final):
    t = x.shape[0]
    return pl.pallas_call(
        functools.partial(_ffn_step_kernel, final=final),
        out_shape=[
            jax.ShapeDtypeStruct((t, D_MODEL), F32),
            jax.ShapeDtypeStruct(buf.shape, F32),
        ],
        scratch_shapes=[pltpu.VMEM((t, FFN_DIM), BF16)],
        compiler_params=pltpu.CompilerParams(vmem_limit_bytes=VMEM_LIMIT),
        name="ffn_step",
    )(x, g, wup, cw, cb, wdn, gf, buf)


def _lru_gates(xc, wa_ref, ba_ref, wx_ref, bx_ref, lam_ref, a_out, u_out):
    xcb = xc.astype(BF16)
    log_s = _log_sigmoid(lam_ref[...])
    for blk in range(LRU_BLOCKS):
        cols = slice(blk * LRU_BW, (blk + 1) * LRU_BW)
        r = jax.nn.sigmoid(_dot(xcb[:, cols], wa_ref[blk]) + ba_ref[:, cols])
        i = jax.nn.sigmoid(_dot(xcb[:, cols], wx_ref[blk]) + bx_ref[:, cols])
        log_a = LRU_C * r * log_s[:, cols]
        a = jnp.exp(log_a)
        a_out[:, cols] = a
        u_out[:, cols] = jnp.sqrt(-jnp.tanh(log_a) * (a * a + 1.0)) * (i * xc[:, cols])


def _lru_kernel(x_ref, g_ref, win_ref, cw_ref, cb_ref, wa_ref, ba_ref, wx_ref, bx_ref, lam_ref, wout_ref,
                xo_ref, h_ref, buf_ref, carry_scr, h_scr, a_scr, u_scr):
    @pl.when(pl.program_id(1) == 0)
    def _():
        carry_scr[...] = jnp.zeros_like(carry_scr)
        h_scr[...] = jnp.zeros_like(h_scr)

    x = x_ref[...]
    tm = x.shape[0]
    hn = _rms(x, g_ref[...]).astype(BF16)
    xb = _dot(hn, win_ref[:, :LRU_WIDTH])
    prev = carry_scr[...]
    xc = cb_ref[...] + _shift_rows(xb, prev, 3) * cw_ref[0:1, :]
    xc = xc + _shift_rows(xb, prev, 2) * cw_ref[1:2, :]
    xc = xc + _shift_rows(xb, prev, 1) * cw_ref[2:3, :]
    xc = xc + xb * cw_ref[3:4, :]
    carry_scr[...] = xb[tm - SUBLANES:, :]
    buf_ref[...] = xb[tm - (LRU_CONV - 1):, :]
    _lru_gates(xc, wa_ref, ba_ref, wx_ref, bx_ref, lam_ref, a_scr, u_scr)

    sub = lax.broadcasted_iota(jnp.int32, (SUBLANES, LRU_WIDTH), 0)

    h = h_scr[...]
    for gi in range(tm // SUBLANES):
        rows = slice(gi * SUBLANES, (gi + 1) * SUBLANES)
        a = a_scr[rows, :]
        u = u_scr[rows, :]
        for s in (1, 2, 4):
            keep = sub >= s
            u = jnp.where(keep, a * pltpu.roll(u, s, 0) + u, u)
            a = jnp.where(keep, a * pltpu.roll(a, s, 0), a)
        hs = a * h + u
        u_scr[rows, :] = hs
        h = hs[SUBLANES - 1:, :]
    h_scr[...] = h
    h_ref[...] = h
    gb = _dot(hn, win_ref[:, LRU_WIDTH:])
    y = (_gelu(gb) * u_scr[...]).astype(BF16)
    xo_ref[...] = x + _dot(y, wout_ref[...])


def _lru_weight_specs():
    return [
        _resident((1, D_MODEL)),
        _resident((D_MODEL, 2 * LRU_WIDTH)),
        _resident((LRU_CONV, LRU_WIDTH)),
        _resident((1, LRU_WIDTH)),
        _resident((LRU_BLOCKS, LRU_BW, LRU_BW)),
        _resident((1, LRU_WIDTH)),
        _resident((LRU_BLOCKS, LRU_BW, LRU_BW)),
        _resident((1, LRU_WIDTH)),
        _resident((1, LRU_WIDTH)),
        _resident((LRU_WIDTH, D_MODEL)),
    ]


def _lru(x, weights, batch, seq, tm):
    t = x.shape[0]
    nj = seq // tm
    return pl.pallas_call(
        _lru_kernel,
        grid=(batch, nj),
        in_specs=[pl.BlockSpec((tm, D_MODEL), lambda b, j: (b * nj + j, 0))] + _lru_weight_specs(),
        out_specs=[
            pl.BlockSpec((tm, D_MODEL), lambda b, j: (b * nj + j, 0)),
            pl.BlockSpec((None, 1, LRU_WIDTH), lambda b, j: (b, 0, 0)),
            pl.BlockSpec((None, LRU_CONV - 1, LRU_WIDTH), lambda b, j: (b, 0, 0)),
        ],
        out_shape=[
            jax.ShapeDtypeStruct((t, D_MODEL), F32),
            jax.ShapeDtypeStruct((batch, 1, LRU_WIDTH), F32),
            jax.ShapeDtypeStruct((batch, LRU_CONV - 1, LRU_WIDTH), F32),
        ],
        scratch_shapes=[
            pltpu.VMEM((SUBLANES, LRU_WIDTH), F32),
            pltpu.VMEM((1, LRU_WIDTH), F32),
            pltpu.VMEM((tm, LRU_WIDTH), F32),
            pltpu.VMEM((tm, LRU_WIDTH), F32),
        ],
        compiler_params=_params("arbitrary", "arbitrary"),
        name="lru",
    )(x, *weights)


def _lru_step_kernel(x_ref, g_ref, win_ref, cw_ref, cb_ref, wa_ref, ba_ref, wx_ref, bx_ref, lam_ref, wout_ref,
                     h0_ref, buf_ref, xo_ref, h_ref, nbuf_ref, a_scr, u_scr):
    x = x_ref[...]
    w = LRU_WIDTH
    hn = _rms(x, g_ref[...]).astype(BF16)
    xb = _dot(hn, win_ref[:, :w])
    xc = cb_ref[...] + buf_ref[:, 0:w] * cw_ref[0:1, :]
    xc = xc + buf_ref[:, w:2 * w] * cw_ref[1:2, :]
    xc = xc + buf_ref[:, 2 * w:3 * w] * cw_ref[2:3, :]
    xc = xc + xb * cw_ref[3:4, :]
    nbuf_ref[:, 0:2 * w] = buf_ref[:, w:3 * w]
    nbuf_ref[:, 2 * w:3 * w] = xb
    _lru_gates(xc, wa_ref, ba_ref, wx_ref, bx_ref, lam_ref, a_scr, u_scr)
    hs = a_scr[...] * h0_ref[...] + u_scr[...]
    h_ref[...] = hs
    gb = _dot(hn, win_ref[:, w:])
    xo_ref[...] = x + _dot((_gelu(gb) * hs).astype(BF16), wout_ref[...])


def _lru_step(x, weights, h0, buf):
    t = x.shape[0]
    return pl.pallas_call(
        _lru_step_kernel,
        out_shape=[
            jax.ShapeDtypeStruct((t, D_MODEL), F32),
            jax.ShapeDtypeStruct(h0.shape, F32),
            jax.ShapeDtypeStruct(buf.shape, F32),
        ],
        scratch_shapes=[pltpu.VMEM((t, LRU_WIDTH), F32), pltpu.VMEM((t, LRU_WIDTH), F32)],
        compiler_params=pltpu.CompilerParams(vmem_limit_bytes=VMEM_LIMIT),
        name="lru_step",
    )(x, *weights, h0, buf)


def _rope_tables(positions):
    inv = ROPE_BASE ** (-jnp.arange(ROPE_HALF, dtype=F32) / ROPE_HALF)
    ang = positions[:, None] * inv[None, :]
    return jnp.cos(ang), jnp.sin(ang)


def kernel(x_prompt, x_sample, state_ret, state_lru_h, state_lru_conv, state_ffn_conv, cache_mem_k, cache_mem_v, mem_prompt, norm_mix, norm_xa, norm_mem, norm_ffn, norm_final, ret_w_in, ret_w_out, lru_w_in, lru_conv_w, lru_conv_b, lru_wa, lru_ba, lru_wx, lru_bx, lru_lambda, lru_w_out, xa_w_q, xa_w_kv, xa_w_o, ffn_w_up, ffn_conv_w, ffn_conv_b, ffn_w_down):
    bp, seq, d = x_prompt.shape
    bs = x_sample.shape[0]
    mem_len = mem_prompt.shape[1]
    assert d == D_MODEL and x_sample.shape[1] == 1
    assert seq % ROW_TILE == 0 and ROW_TILE % SCAN_CHUNK == 0 and bs % STEP_ROWS == 0

    row = lambda v: v.reshape(1, -1)
    ret_w_in_b = ret_w_in[0].astype(BF16)
    ret_w_out_b = ret_w_out[0].astype(BF16)
    xa_w_q_b = xa_w_q.astype(BF16)
    xa_w_kv_b = xa_w_kv.astype(BF16)
    xa_w_o_b = xa_w_o.astype(BF16)
    ffn_w_up_b = ffn_w_up.astype(BF16)
    ffn_w_down_b = ffn_w_down.astype(BF16)
    lru_weights = lambda j: (
        row(norm_mix[1]), lru_w_in[j].astype(BF16), lru_conv_w[j], row(lru_conv_b[j]),
        lru_wa[j].astype(BF16), row(lru_ba[j]), lru_wx[j].astype(BF16), row(lru_bx[j]),
        row(lru_lambda[j]), lru_w_out[j].astype(BF16))
    gf = row(norm_final)

    mem_k, mem_v = _mem_kv(mem_prompt.reshape(bp * mem_len, d), norm_mem.reshape(DEPTH, 1, d), xa_w_kv_b, ROW_TILE)
    mem_k = mem_k.reshape(DEPTH, bp, mem_len, d)
    mem_v = mem_v.reshape(DEPTH, bp, mem_len, d)
    cos_p, sin_p = _rope_tables(jnp.arange(seq, dtype=F32))
    xp = x_prompt.reshape(bp * seq, d)
    qkvg = _ret_proj(xp, row(norm_mix[0]), ret_w_in_b, cos_p, sin_p, ROW_TILE, BF16)
    xp, ret_p = _ret_scan(qkvg, xp, ret_w_out_b, bp, seq, ROW_TILE)
    xp = _xattn(xp, row(norm_xa[0]), xa_w_q_b[0], mem_k, mem_v, xa_w_o_b[0], 0, bp, seq, ROW_TILE)
    xp, ffn_p0 = _ffn(xp, row(norm_ffn[0]), ffn_w_up_b[0], ffn_conv_w[0], row(ffn_conv_b[0]), ffn_w_down_b[0],
                      gf, bp, seq, ROW_TILE, False)
    xp, lru_h_p, lru_conv_p = _lru(xp, lru_weights(0), bp, seq, ROW_TILE)
    xp = _xattn(xp, row(norm_xa[1]), xa_w_q_b[1], mem_k, mem_v, xa_w_o_b[1], 1, bp, seq, ROW_TILE)
    xp, ffn_p1 = _ffn(xp, row(norm_ffn[1]), ffn_w_up_b[1], ffn_conv_w[1], row(ffn_conv_b[1]), ffn_w_down_b[1],
                      gf, bp, seq, ROW_TILE, True)

    cos_s, sin_s = _rope_tables(jnp.full((bs,), PAST_LEN, F32))
    heads = lambda v: v.reshape(bs, XA_HEADS, XA_HD)
    xs = x_sample.reshape(bs, d)
    qkvg_s = _ret_proj(xs, row(norm_mix[0]), ret_w_in_b, cos_s, sin_s, bs, F32)
    ret_s, y_s = _ret_step(qkvg_s, state_ret)
    xs = _matmul_res(y_s, ret_w_out_b, xs)
    q_s = heads(_norm_matmul(xs, row(norm_xa[0]), xa_w_q_b[0]))
    xs = _matmul_res(_xattn_step(q_s, cache_mem_k, cache_mem_v, 0).reshape(bs, d), xa_w_o_b[0], xs)
    xs, ffn_s0 = _ffn_step(xs, row(norm_ffn[0]), ffn_w_up_b[0], ffn_conv_w[0], row(ffn_conv_b[0]), ffn_w_down_b[0],
                           gf, state_ffn_conv[0].reshape(bs, -1), False)
    xs, lru_h_s, lru_conv_s = _lru_step(xs, lru_weights(0), state_lru_h[0], state_lru_conv[0].reshape(bs, -1))
    q_s = heads(_norm_matmul(xs, row(norm_xa[1]), xa_w_q_b[1]))
    xs = _matmul_res(_xattn_step(q_s, cache_mem_k, cache_mem_v, 1).reshape(bs, d), xa_w_o_b[1], xs)
    xs, ffn_s1 = _ffn_step(xs, row(norm_ffn[1]), ffn_w_up_b[1], ffn_conv_w[1], row(ffn_conv_b[1]), ffn_w_down_b[1],
                           gf, state_ffn_conv[1].reshape(bs, -1), True)

    mem_shape = (DEPTH, bp, mem_len, XA_HEADS, XA_HD)
    return (
        xp.reshape(bp, seq, d),
        xs.reshape(bs, 1, d),
        ret_p[None],
        ret_s,
        lru_h_p.reshape(1, bp, LRU_WIDTH),
        lru_h_s[None],
        lru_conv_p[None],
        lru_conv_s.reshape(1, bs, LRU_CONV - 1, LRU_WIDTH),
        jnp.stack([ffn_p0, ffn_p1]),
        jnp.stack([ffn_s0, ffn_s1]).reshape(DEPTH, bs, FFN_CONV - 1, FFN_DIM),
        mem_k.reshape(mem_shape),
        mem_v.reshape(mem_shape),
    )
```

```python
import functools
import math

import jax
import jax.numpy as jnp
from jax import lax
from jax.experimental import pallas as pl
from jax.experimental.pallas import tpu as pltpu

F32 = jnp.float32
BF16 = jnp.bfloat16

D_MODEL = 1024
DEPTH = 2
PAST_LEN = 16384
RET_HEADS = 4
RET_DK = D_MODEL // RET_HEADS
RET_DV = 2 * D_MODEL // RET_HEADS
ROPE_BASE = 10000.0
ROPE_HALF = RET_DK // 2
LRU_WIDTH = D_MODEL
LRU_BLOCKS = 4
LRU_BW = LRU_WIDTH // LRU_BLOCKS
LRU_CONV = 4
LRU_C = 8.0
XA_HEADS = 4
XA_HD = D_MODEL // XA_HEADS
FFN_DIM = 3 * D_MODEL
FFN_CONV = 3
EPS = 1e-6

RET_QK = RET_HEADS * RET_DK
RET_V = RET_HEADS * RET_DV
RET_COLS = 2 * RET_QK + 2 * RET_V
RET_GAMMA = tuple(1.0 - 2.0 ** (-5.0 - h) for h in range(RET_HEADS))
RET_LOG_G = tuple(math.log(g) for g in RET_GAMMA)
K_SCALE = RET_DK ** -0.5
XA_SCALE = XA_HD ** -0.5
SQRT_2_OVER_PI = math.sqrt(2.0 / math.pi)

SUBLANES = 8
VMEM_LIMIT = 56 << 20

ROW_TILE = 512
SCAN_CHUNK = 256
FFN_COLS = 512
STEP_ROWS = 8


def _params(*sem):
    return pltpu.CompilerParams(dimension_semantics=sem, vmem_limit_bytes=VMEM_LIMIT)


def _resident(shape):
    zeros = (0,) * len(shape)
    return pl.BlockSpec(shape, lambda *_: zeros, pipeline_mode=pl.Buffered(1))


def _dot(a, b):
    return jnp.dot(a, b, preferred_element_type=F32)


def _dot_nt(a, b):
    return lax.dot_general(a, b, (((1,), (1,)), ((), ())), preferred_element_type=F32)


def _rms(x, g):
    return x * lax.rsqrt(jnp.mean(x * x, axis=-1, keepdims=True) + EPS) * g


def _gelu(x):
    return x * (0.5 * (1.0 + jnp.tanh(SQRT_2_OVER_PI * (x + 0.044715 * (x * x * x)))))


def _silu(x):
    return x * jax.nn.sigmoid(x)


def _log_sigmoid(x):
    return jnp.minimum(x, 0.0) - jnp.log(1.0 + jnp.exp(-jnp.abs(x)))


def _shift_rows(x, prev, s):
    rolled = pltpu.roll(x, s, 0)
    rows = lax.broadcasted_iota(jnp.int32, (SUBLANES, 1), 0)
    head = jnp.where(rows < s, pltpu.roll(prev, s, 0), rolled[:SUBLANES])
    return jnp.concatenate([head, rolled[SUBLANES:]], axis=0)


def _ret_proj_kernel(x_ref, g_ref, w_ref, cos_ref, sin_ref, o_ref):
    hn = _rms(x_ref[...], g_ref[...]).astype(BF16)
    cos = cos_ref[...]
    sin = sin_ref[...]
    for n in range(2 * RET_HEADS):
        c0 = n * RET_DK
        acc = _dot(hn, w_ref[:, c0:c0 + RET_DK])
        x1 = acc[:, :ROPE_HALF]
        x2 = acc[:, ROPE_HALF:]
        r1 = x1 * cos - x2 * sin
        r2 = x1 * sin + x2 * cos
        if n >= RET_HEADS:
            r1 = r1 * K_SCALE
            r2 = r2 * K_SCALE
        o_ref[:, c0:c0 + ROPE_HALF] = r1.astype(o_ref.dtype)
        o_ref[:, c0 + ROPE_HALF:c0 + RET_DK] = r2.astype(o_ref.dtype)
    for n in range(2 * RET_V // RET_DV):
        c0 = 2 * RET_QK + n * RET_DV
        o_ref[:, c0:c0 + RET_DV] = _dot(hn, w_ref[:, c0:c0 + RET_DV]).astype(o_ref.dtype)


def _ret_proj(x, g, w, cos, sin, tm, out_dtype):
    t = x.shape[0]
    period = cos.shape[0] // tm
    return pl.pallas_call(
        _ret_proj_kernel,
        grid=(t // tm,),
        in_specs=[
            pl.BlockSpec((tm, D_MODEL), lambda m: (m, 0)),
            _resident((1, D_MODEL)),
            _resident((D_MODEL, RET_COLS)),
            pl.BlockSpec((tm, ROPE_HALF), lambda m: (m % period, 0)),
            pl.BlockSpec((tm, ROPE_HALF), lambda m: (m % period, 0)),
        ],
        out_specs=pl.BlockSpec((tm, RET_COLS), lambda m: (m, 0)),
        out_shape=jax.ShapeDtypeStruct((t, RET_COLS), out_dtype),
        compiler_params=_params("arbitrary"),
        name="ret_proj",
    )(x, g, w, cos, sin)


def _ret_scan_kernel(q_ref, k_ref, v_ref, g_ref, x_ref, w_ref, xo_ref, s_ref, y_scr, intra_scr):
    c = SCAN_CHUNK

    @pl.when((pl.program_id(0) == 0) & (pl.program_id(1) == 0))
    def _():
        row = lax.broadcasted_iota(jnp.int32, (c, c), 0).astype(F32)
        col = lax.broadcasted_iota(jnp.int32, (c, c), 1).astype(F32)
        rel = row - col
        for h in range(RET_HEADS):
            intra_scr[h] = jnp.where(rel >= 0, jnp.exp(RET_LOG_G[h] * jnp.maximum(rel, 0.0)), 0.0)

    @pl.when(pl.program_id(1) == 0)
    def _():
        s_ref[...] = jnp.zeros_like(s_ref)

    idx = lax.broadcasted_iota(jnp.int32, (c, 1), 0).astype(F32)
    for h in range(RET_HEADS):
        lg = RET_LOG_G[h]
        intra = intra_scr[h]
        q_dec = jnp.exp(lg * (idx + 1.0))
        k_dec = jnp.exp(lg * (c - 1.0 - idx))
        chunk_dec = math.exp(lg * c)
        qk_cols = slice(h * RET_DK, (h + 1) * RET_DK)
        v_cols = slice(h * RET_DV, (h + 1) * RET_DV)
        for ci in range(q_ref.shape[0] // c):
            rows = slice(ci * c, (ci + 1) * c)
            qc = q_ref[rows, qk_cols]
            kc = k_ref[rows, qk_cols]
            vc = v_ref[rows, v_cols]
            s = s_ref[h]
            att = _dot_nt(qc, kc) * intra
            o = _dot(att.astype(BF16), vc) + _dot((qc.astype(F32) * q_dec).astype(BF16), s.astype(BF16))
            kd_t = (kc.astype(F32) * k_dec).T.astype(BF16)
            s_ref[h] = s * chunk_dec + _dot(kd_t, vc)
            o = o * lax.rsqrt(jnp.mean(o * o, axis=-1, keepdims=True) + EPS)
            y_scr[rows, v_cols] = (_silu(g_ref[rows, v_cols].astype(F32)) * o).astype(BF16)
    xo_ref[...] = x_ref[...] + _dot(y_scr[...], w_ref[...])


def _ret_scan(qkvg, x, w_out, batch, seq, tm):
    t = x.shape[0]
    nj = seq // tm
    return pl.pallas_call(
        _ret_scan_kernel,
        grid=(batch, nj),
        in_specs=[
            pl.BlockSpec((tm, RET_QK), lambda b, j: (b * nj + j, 0)),
            pl.BlockSpec((tm, RET_QK), lambda b, j: (b * nj + j, 1)),
            pl.BlockSpec((tm, RET_V), lambda b, j: (b * nj + j, 1)),
            pl.BlockSpec((tm, RET_V), lambda b, j: (b * nj + j, 2)),
            pl.BlockSpec((tm, D_MODEL), lambda b, j: (b * nj + j, 0)),
            _resident((RET_V, D_MODEL)),
        ],
        out_specs=[
            pl.BlockSpec((tm, D_MODEL), lambda b, j: (b * nj + j, 0)),
            pl.BlockSpec((None, RET_HEADS, RET_DK, RET_DV), lambda b, j: (b, 0, 0, 0)),
        ],
        out_shape=[
            jax.ShapeDtypeStruct((t, D_MODEL), F32),
            jax.ShapeDtypeStruct((batch, RET_HEADS, RET_DK, RET_DV), F32),
        ],
        scratch_shapes=[
            pltpu.VMEM((tm, RET_V), BF16),
            pltpu.VMEM((RET_HEADS, SCAN_CHUNK, SCAN_CHUNK), F32),
        ],
        compiler_params=_params("arbitrary", "arbitrary"),
        name="ret_scan",
    )(qkvg, qkvg, qkvg, qkvg, x, w_out)


def _ret_rows(q_ref, k_ref, v_ref, g_ref, s_ref, so_ref, y_ref):
    n = q_ref.shape[0]
    pad = jnp.zeros((SUBLANES - n, RET_DK), F32)
    v = v_ref[...]
    gate = _silu(g_ref[...])
    for h in range(RET_HEADS):
        qk_cols = slice(h * RET_DK, (h + 1) * RET_DK)
        v_cols = slice(h * RET_DV, (h + 1) * RET_DV)
        q_t = jnp.concatenate([q_ref[:, qk_cols], pad], axis=0).T
        k_t = jnp.concatenate([k_ref[:, qk_cols], pad], axis=0).T
        for i in range(n):
            s_new = s_ref[i, h] * RET_GAMMA[h] + k_t[:, i:i + 1] * v[i:i + 1, v_cols]
            so_ref[i, h] = s_new
            o = jnp.sum(q_t[:, i:i + 1] * s_new, axis=0, keepdims=True)
            o = o * lax.rsqrt(jnp.mean(o * o, axis=-1, keepdims=True) + EPS)
            y_ref[i:i + 1, v_cols] = gate[i:i + 1, v_cols] * o


def _ret_rows_specs(rows, step):
    state = pl.BlockSpec((None, rows, RET_HEADS, RET_DK, RET_DV), lambda *ids: (0, step(*ids), 0, 0, 0))
    ins = [
        pl.BlockSpec((None, rows, RET_QK), lambda *ids: (step(*ids), 0, 0)),
        pl.BlockSpec((None, rows, RET_QK), lambda *ids: (step(*ids), 0, 1)),
        pl.BlockSpec((None, rows, RET_V), lambda *ids: (step(*ids), 0, 1)),
        pl.BlockSpec((None, rows, RET_V), lambda *ids: (step(*ids), 0, 2)),
        state,
    ]
    outs = [state, pl.BlockSpec((None, rows, RET_V), lambda *ids: (step(*ids), 0, 0))]
    return ins, outs


def _norm_matmul_kernel(x_ref, g_ref, w_ref, o_ref):
    o_ref[...] = _dot(_rms(x_ref[...], g_ref[...]).astype(BF16), w_ref[...])


def _norm_matmul(x, g, w):
    return pl.pallas_call(
        _norm_matmul_kernel,
        out_shape=jax.ShapeDtypeStruct((x.shape[0], w.shape[1]), F32),
        compiler_params=pltpu.CompilerParams(vmem_limit_bytes=VMEM_LIMIT),
        name="norm_matmul",
    )(x, g, w)


def _matmul_res_kernel(y_ref, w_ref, x_ref, o_ref):
    o_ref[...] = x_ref[...] + _dot(y_ref[...].astype(BF16), w_ref[...])


def _matmul_res(y, w, x):
    return pl.pallas_call(
        _matmul_res_kernel,
        out_shape=jax.ShapeDtypeStruct(x.shape, F32),
        compiler_params=pltpu.CompilerParams(vmem_limit_bytes=VMEM_LIMIT),
        name="matmul_res",
    )(y, w, x)


def _mem_kv_kernel(m_ref, g_ref, w_ref, k_ref, v_ref):
    hn = _rms(m_ref[...], g_ref[...]).astype(BF16)
    k_ref[...] = _dot(hn, w_ref[:, :D_MODEL])
    v_ref[...] = _dot(hn, w_ref[:, D_MODEL:])


def _mem_kv(mem, g, w, tm):
    t = mem.shape[0]
    out = jax.ShapeDtypeStruct((DEPTH, t, D_MODEL), F32)
    return pl.pallas_call(
        _mem_kv_kernel,
        grid=(DEPTH, t // tm),
        in_specs=[
            pl.BlockSpec((tm, D_MODEL), lambda i, m: (m, 0)),
            pl.BlockSpec((None, 1, D_MODEL), lambda i, m: (i, 0, 0)),
            pl.BlockSpec((None, D_MODEL, 2 * D_MODEL), lambda i, m: (i, 0, 0)),
        ],
        out_specs=[
            pl.BlockSpec((None, tm, D_MODEL), lambda i, m: (i, m, 0)),
            pl.BlockSpec((None, tm, D_MODEL), lambda i, m: (i, m, 0)),
        ],
        out_shape=[out, out],
        compiler_params=_params("arbitrary", "arbitrary"),
        name="mem_kv",
    )(mem, g, w)


def _xattn_rows_specs(rows, mem_len, layer, step):
    row = pl.BlockSpec((rows, XA_HEADS, XA_HD), lambda *ids: (step(*ids), 0, 0))
    mem = pl.BlockSpec((None, rows, mem_len, XA_HEADS, XA_HD), lambda *ids: (layer, step(*ids), 0, 0, 0))
    return [row, mem, mem], row


def _xattn_row(q_ref, mk_ref, mv_ref, o_ref, i):
    s = jnp.sum(mk_ref[i] * q_ref[i][None], axis=-1, keepdims=True) * XA_SCALE
    e = jnp.exp(s - jnp.max(s, axis=0, keepdims=True))
    p = e / jnp.sum(e, axis=0, keepdims=True)
    o_ref[i] = jnp.sum(p * mv_ref[i], axis=0)


def _xattn_step_kernel(q_ref, mk_ref, mv_ref, o_ref):
    for i in range(q_ref.shape[0]):
        _xattn_row(q_ref, mk_ref, mv_ref, o_ref, i)


def _xattn_step(q, mk, mv, layer):
    ins, out = _xattn_rows_specs(STEP_ROWS, mk.shape[2], layer, lambda i: i)
    return pl.pallas_call(
        _xattn_step_kernel,
        grid=(q.shape[0] // STEP_ROWS,),
        in_specs=ins,
        out_specs=out,
        out_shape=jax.ShapeDtypeStruct(q.shape, F32),
        compiler_params=_params("arbitrary"),
        name="xattn_step",
    )(q, mk, mv)


def _xattn_main(x_ref, g_ref, wq_ref, mk_ref, mv_ref, wo_ref, xo_ref, o_scr, per_head=None):
    x = x_ref[...]
    q = _dot(_rms(x, g_ref[...]).astype(BF16), wq_ref[...])
    for h in range(XA_HEADS):
        cols = slice(h * XA_HD, (h + 1) * XA_HD)
        s = _dot_nt(q[:, cols].astype(BF16), mk_ref[:, cols].astype(BF16)) * XA_SCALE
        e = jnp.exp(s - jnp.max(s, axis=-1, keepdims=True))
        p = e / jnp.sum(e, axis=-1, keepdims=True)
        o_scr[:, cols] = _dot(p.astype(BF16), mv_ref[:, cols].astype(BF16)).astype(BF16)
        if per_head is not None:
            per_head(h)
    xo_ref[...] = x + _dot(o_scr[...], wo_ref[...])


def _xattn_kernel(x_ref, g_ref, wq_ref, mk_ref, mv_ref, wo_ref, xo_ref, o_scr):
    _xattn_main(x_ref, g_ref, wq_ref, mk_ref, mv_ref, wo_ref, xo_ref, o_scr)


def _xattn_ret_kernel(x_ref, g_ref, wq_ref, mk_ref, mv_ref, wo_ref, q_ref, k_ref, v_ref, gt_ref, s_ref,
                      xo_ref, so_ref, y_ref, o_scr):
    _xattn_main(x_ref, g_ref, wq_ref, mk_ref, mv_ref, wo_ref, xo_ref, o_scr)
    _ret_rows(q_ref, k_ref, v_ref, gt_ref, s_ref, so_ref, y_ref)


def _xattn_mem_kernel(x_ref, g_ref, wq_ref, mk_ref, mv_ref, wo_ref, qs_ref, ck_ref, cv_ref,
                      xo_ref, os_ref, o_scr):
    rows = qs_ref.shape[0]
    assert rows <= XA_HEADS

    def per_head(h):
        if h < rows:
            _xattn_row(qs_ref, ck_ref, cv_ref, os_ref, h)

    _xattn_main(x_ref, g_ref, wq_ref, mk_ref, mv_ref, wo_ref, xo_ref, o_scr, per_head)


def _xattn(x, g, wq, mk, mv, wo, layer, batch, seq, tm, ret_step=None, mem_step=None):
    t = x.shape[0]
    nj = seq // tm
    mem_len = mk.shape[2]
    step = lambda b, j: b * nj + j
    mem_spec = pl.BlockSpec((None, None, mem_len, D_MODEL), lambda b, j: (layer, b, 0, 0))
    row_spec = pl.BlockSpec((tm, D_MODEL), lambda b, j: (step(b, j), 0))
    in_specs = [row_spec, _resident((1, D_MODEL)), _resident((D_MODEL, D_MODEL)), mem_spec, mem_spec,
                _resident((D_MODEL, D_MODEL))]
    out_specs = [row_spec]
    out_shape = [jax.ShapeDtypeStruct((t, D_MODEL), F32)]
    args = [x, g, wq, mk, mv, wo]
    body = _xattn_kernel
    assert ret_step is None or mem_step is None
    if ret_step is not None:
        qkvg, state = ret_step
        steps, rows, _ = qkvg.shape
        assert steps == batch * nj and steps * rows == state.shape[1]
        ret_in, ret_out = _ret_rows_specs(rows, step)
        in_specs += ret_in
        out_specs += ret_out
        out_shape += [jax.ShapeDtypeStruct(state.shape, F32), jax.ShapeDtypeStruct((steps, rows, RET_V), F32)]
        args += [qkvg, qkvg, qkvg, qkvg, state]
        body = _xattn_ret_kernel
    if mem_step is not None:
        q_s, cache_k, cache_v, cache_layer = mem_step
        rows = q_s.shape[0] // (batch * nj)
        assert rows * batch * nj == q_s.shape[0]
        side_in, side_out = _xattn_rows_specs(rows, cache_k.shape[2], cache_layer, step)
        in_specs += side_in
        out_specs += [side_out]
        out_shape += [jax.ShapeDtypeStruct(q_s.shape, F32)]
        args += [q_s, cache_k, cache_v]
        body = _xattn_mem_kernel
    return pl.pallas_call(
        body,
        grid=(batch, nj),
        in_specs=in_specs,
        out_specs=out_specs,
        out_shape=out_shape,
        scratch_shapes=[pltpu.VMEM((tm, D_MODEL), BF16)],
        compiler_params=_params("arbitrary", "arbitrary"),
        name="xattn",
    )(*args)


def _ffn_tail(x, y_scr, wdn_ref, gf_ref, xo_ref, final):
    out = x + _dot(y_scr[...], wdn_ref[...])
    if final:
        out = _rms(out, gf_ref[...])
    xo_ref[...] = out


def _ffn_kernel(x_ref, g_ref, wup_ref, cw_ref, cb_ref, wdn_ref, gf_ref, xo_ref, buf_ref,
                carry_scr, y_scr, *, final):
    @pl.when(pl.program_id(1) == 0)
    def _():
        carry_scr[...] = jnp.zeros_like(carry_scr)

    x = x_ref[...]
    tm = x.shape[0]
    hn = _rms(x, g_ref[...]).astype(BF16)
    for n in range(FFN_DIM // FFN_COLS):
        cols = slice(n * FFN_COLS, (n + 1) * FFN_COLS)
        gcols = slice(FFN_DIM + n * FFN_COLS, FFN_DIM + (n + 1) * FFN_COLS)
        u = _dot(hn, wup_ref[:, cols])
        gate = _dot(hn, wup_ref[:, gcols])
        prev = carry_scr[:, cols]
        uc = cb_ref[:, cols] + _shift_rows(u, prev, 2) * cw_ref[0:1, cols]
        uc = uc + _shift_rows(u, prev, 1) * cw_ref[1:2, cols]
        uc = uc + u * cw_ref[2:3, cols]
        y_scr[:, cols] = (_gelu(uc) * gate).astype(BF16)
        carry_scr[:, cols] = u[tm - SUBLANES:, :]
        buf_ref[:, cols] = u[tm - (FFN_CONV - 1):, :]
    _ffn_tail(x, y_scr, wdn_ref, gf_ref, xo_ref, final)


def _ffn(x, g, wup, cw, cb, wdn, gf, batch, seq, tm, final):
    t = x.shape[0]
    nj = seq // tm
    step = lambda b, j: b * nj + j
    return pl.pallas_call(
        functools.partial(_ffn_kernel, final=final),
        grid=(batch, nj),
        in_specs=[
            pl.BlockSpec((tm, D_MODEL), lambda b, j: (step(b, j), 0)),
            _resident((1, D_MODEL)),
            _resident((D_MODEL, 2 * FFN_DIM)),
            _resident((FFN_CONV, FFN_DIM)),
            _resident((1, FFN_DIM)),
            _resident((FFN_DIM, D_MODEL)),
            _resident((1, D_MODEL)),
        ],
        out_specs=[
            pl.BlockSpec((tm, D_MODEL), lambda b, j: (step(b, j), 0)),
            pl.BlockSpec((None, FFN_CONV - 1, FFN_DIM), lambda b, j: (b, 0, 0)),
        ],
        out_shape=[
            jax.ShapeDtypeStruct((t, D_MODEL), F32),
            jax.ShapeDtypeStruct((batch, FFN_CONV - 1, FFN_DIM), F32),
        ],
        scratch_shapes=[pltpu.VMEM((SUBLANES, FFN_DIM), F32), pltpu.VMEM((tm, FFN_DIM), BF16)],
        compiler_params=_params("arbitrary", "arbitrary"),
        name="ffn",
    )(x, g, wup, cw, cb, wdn, gf)


def _ffn_step_kernel(x_ref, g_ref, wup_ref, cw_ref, cb_ref, wdn_ref, gf_ref, buf_ref, xo_ref, nbuf_ref,
                     y_scr, *, final):
    x = x_ref[...]
    hn = _rms(x, g_ref[...]).astype(BF16)
    for n in range(FFN_DIM // FFN_COLS):
        cols = slice(n * FFN_COLS, (n + 1) * FFN_COLS)
        gcols = slice(FFN_DIM + n * FFN_COLS, FFN_DIM + (n + 1) * FFN_COLS)
        u = _dot(hn, wup_ref[:, cols])
        gate = _dot(hn, wup_ref[:, gcols])
        b0 = buf_ref[:, cols]
        b1 = buf_ref[:, gcols]
        uc = cb_ref[:, cols] + b0 * cw_ref[0:1, cols]
        uc = uc + b1 * cw_ref[1:2, cols]
        uc = uc + u * cw_ref[2:3, cols]
        y_scr[:, cols] = (_gelu(uc) * gate).astype(BF16)
        nbuf_ref[:, cols] = b1
        nbuf_ref[:, gcols] = u
    _ffn_tail(x, y_scr, wdn_ref, gf_ref, xo_ref, final)


def _ffn_step(x, g, wup, cw, cb, wdn, gf, buf, final):
    t = x.shape[0]
    return pl.pallas_call(
        functools.partial(_ffn_step_kernel, final=final),
        out_shape=[
            jax.ShapeDtypeStruct((t, D_MODEL), F32),
            jax.ShapeDtypeStruct(buf.shape, F32),
        ],
        scratch_shapes=[pltpu.VMEM((t, FFN_DIM), BF16)],
        compiler_params=pltpu.CompilerParams(vmem_limit_bytes=VMEM_LIMIT),
        name="ffn_step",
    )(x, g, wup, cw, cb, wdn, gf, buf)


def _lru_gates(xc, wa_ref, ba_ref, wx_ref, bx_ref, lam_ref, a_out, u_out):
    xcb = xc.astype(BF16)
    log_s = _log_sigmoid(lam_ref[...])
    for blk in range(LRU_BLOCKS):
        cols = slice(blk * LRU_BW, (blk + 1) * LRU_BW)
        r = jax.nn.sigmoid(_dot(xcb[:, cols], wa_ref[blk]) + ba_ref[:, cols])
        i = jax.nn.sigmoid(_dot(xcb[:, cols], wx_ref[blk]) + bx_ref[:, cols])
        log_a = LRU_C * r * log_s[:, cols]
        a = jnp.exp(log_a)
        a_out[:, cols] = a
        u_out[:, cols] = jnp.sqrt(-jnp.tanh(log_a) * (a * a + 1.0)) * (i * xc[:, cols])


def _lru_kernel(x_ref, g_ref, win_ref, cw_ref, cb_ref, wa_ref, ba_ref, wx_ref, bx_ref, lam_ref, wout_ref,
                xo_ref, h_ref, buf_ref, carry_scr, h_scr, a_scr, u_scr):
    @pl.when(pl.program_id(1) == 0)
    def _():
        carry_scr[...] = jnp.zeros_like(carry_scr)
        h_scr[...] = jnp.zeros_like(h_scr)

    x = x_ref[...]
    tm = x.shape[0]
    hn = _rms(x, g_ref[...]).astype(BF16)
    xb = _dot(hn, win_ref[:, :LRU_WIDTH])
    prev = carry_scr[...]
    xc = cb_ref[...] + _shift_rows(xb, prev, 3) * cw_ref[0:1, :]
    xc = xc + _shift_rows(xb, prev, 2) * cw_ref[1:2, :]
    xc = xc + _shift_rows(xb, prev, 1) * cw_ref[2:3, :]
    xc = xc + xb * cw_ref[3:4, :]
    carry_scr[...] = xb[tm - SUBLANES:, :]
    buf_ref[...] = xb[tm - (LRU_CONV - 1):, :]
    _lru_gates(xc, wa_ref, ba_ref, wx_ref, bx_ref, lam_ref, a_scr, u_scr)

    sub = lax.broadcasted_iota(jnp.int32, (SUBLANES, LRU_WIDTH), 0)

    h = h_scr[...]
    for gi in range(tm // SUBLANES):
        rows = slice(gi * SUBLANES, (gi + 1) * SUBLANES)
        a = a_scr[rows, :]
        u = u_scr[rows, :]
        for s in (1, 2, 4):
            keep = sub >= s
            u = jnp.where(keep, a * pltpu.roll(u, s, 0) + u, u)
            a = jnp.where(keep, a * pltpu.roll(a, s, 0), a)
        hs = a * h + u
        u_scr[rows, :] = hs
        h = hs[SUBLANES - 1:, :]
    h_scr[...] = h
    h_ref[...] = h
    gb = _dot(hn, win_ref[:, LRU_WIDTH:])
    y = (_gelu(gb) * u_scr[...]).astype(BF16)
    xo_ref[...] = x + _dot(y, wout_ref[...])


def _lru_weight_specs():
    return [
        _resident((1, D_MODEL)),
        _resident((D_MODEL, 2 * LRU_WIDTH)),
        _resident((LRU_CONV, LRU_WIDTH)),
        _resident((1, LRU_WIDTH)),
        _resident((LRU_BLOCKS, LRU_BW, LRU_BW)),
        _resident((1, LRU_WIDTH)),
        _resident((LRU_BLOCKS, LRU_BW, LRU_BW)),
        _resident((1, LRU_WIDTH)),
        _resident((1, LRU_WIDTH)),
        _resident((LRU_WIDTH, D_MODEL)),
    ]


def _lru(x, weights, batch, seq, tm):
    t = x.shape[0]
    nj = seq // tm
    return pl.pallas_call(
        _lru_kernel,
        grid=(batch, nj),
        in_specs=[pl.BlockSpec((tm, D_MODEL), lambda b, j: (b * nj + j, 0))] + _lru_weight_specs(),
        out_specs=[
            pl.BlockSpec((tm, D_MODEL), lambda b, j: (b * nj + j, 0)),
            pl.BlockSpec((None, 1, LRU_WIDTH), lambda b, j: (b, 0, 0)),
            pl.BlockSpec((None, LRU_CONV - 1, LRU_WIDTH), lambda b, j: (b, 0, 0)),
        ],
        out_shape=[
            jax.ShapeDtypeStruct((t, D_MODEL), F32),
            jax.ShapeDtypeStruct((batch, 1, LRU_WIDTH), F32),
            jax.ShapeDtypeStruct((batch, LRU_CONV - 1, LRU_WIDTH), F32),
        ],
        scratch_shapes=[
            pltpu.VMEM((SUBLANES, LRU_WIDTH), F32),
            pltpu.VMEM((1, LRU_WIDTH), F32),
            pltpu.VMEM((tm, LRU_WIDTH), F32),
            pltpu.VMEM((tm, LRU_WIDTH), F32),
        ],
        compiler_params=_params("arbitrary", "arbitrary"),
        name="lru",
    )(x, *weights)


def _lru_step_kernel(x_ref, g_ref, win_ref, cw_ref, cb_ref, wa_ref, ba_ref, wx_ref, bx_ref, lam_ref, wout_ref,
                     h0_ref, buf_ref, xo_ref, h_ref, nbuf_ref, a_scr, u_scr):
    x = x_ref[...]
    w = LRU_WIDTH
    hn = _rms(x, g_ref[...]).astype(BF16)
    xb = _dot(hn, win_ref[:, :w])
    xc = cb_ref[...] + buf_ref[:, 0:w] * cw_ref[0:1, :]
    xc = xc + buf_ref[:, w:2 * w] * cw_ref[1:2, :]
    xc = xc + buf_ref[:, 2 * w:3 * w] * cw_ref[2:3, :]
    xc = xc + xb * cw_ref[3:4, :]
    nbuf_ref[:, 0:2 * w] = buf_ref[:, w:3 * w]
    nbuf_ref[:, 2 * w:3 * w] = xb
    _lru_gates(xc, wa_ref, ba_ref, wx_ref, bx_ref, lam_ref, a_scr, u_scr)
    hs = a_scr[...] * h0_ref[...] + u_scr[...]
    h_ref[...] = hs
    gb = _dot(hn, win_ref[:, w:])
    xo_ref[...] = x + _dot((_gelu(gb) * hs).astype(BF16), wout_ref[...])


def _lru_step(x, weights, h0, buf):
    t = x.shape[0]
    return pl.pallas_call(
        _lru_step_kernel,
        out_shape=[
            jax.ShapeDtypeStruct((t, D_MODEL), F32),
            jax.ShapeDtypeStruct(h0.shape, F32),
            jax.ShapeDtypeStruct(buf.shape, F32),
        ],
        scratch_shapes=[pltpu.VMEM((t, LRU_WIDTH), F32), pltpu.VMEM((t, LRU_WIDTH), F32)],
        compiler_params=pltpu.CompilerParams(vmem_limit_bytes=VMEM_LIMIT),
        name="lru_step",
    )(x, *weights, h0, buf)


def _rope_tables(positions):
    inv = ROPE_BASE ** (-jnp.arange(ROPE_HALF, dtype=F32) / ROPE_HALF)
    ang = positions[:, None] * inv[None, :]
    return jnp.cos(ang), jnp.sin(ang)


def kernel(x_prompt, x_sample, state_ret, state_lru_h, state_lru_conv, state_ffn_conv, cache_mem_k, cache_mem_v, mem_prompt, norm_mix, norm_xa, norm_mem, norm_ffn, norm_final, ret_w_in, ret_w_out, lru_w_in, lru_conv_w, lru_conv_b, lru_wa, lru_ba, lru_wx, lru_bx, lru_lambda, lru_w_out, xa_w_q, xa_w_kv, xa_w_o, ffn_w_up, ffn_conv_w, ffn_conv_b, ffn_w_down):
    bp, seq, d = x_prompt.shape
    bs = x_sample.shape[0]
    mem_len = mem_prompt.shape[1]
    assert d == D_MODEL and x_sample.shape[1] == 1
    assert seq % ROW_TILE == 0 and ROW_TILE % SCAN_CHUNK == 0 and bs % STEP_ROWS == 0

    row = lambda v: v.reshape(1, -1)
    ret_w_in_b = ret_w_in[0].astype(BF16)
    ret_w_out_b = ret_w_out[0].astype(BF16)
    xa_w_q_b = xa_w_q.astype(BF16)
    xa_w_kv_b = xa_w_kv.astype(BF16)
    xa_w_o_b = xa_w_o.astype(BF16)
    ffn_w_up_b = ffn_w_up.astype(BF16)
    ffn_w_down_b = ffn_w_down.astype(BF16)
    lru_weights = lambda j: (
        row(norm_mix[1]), lru_w_in[j].astype(BF16), lru_conv_w[j], row(lru_conv_b[j]),
        lru_wa[j].astype(BF16), row(lru_ba[j]), lru_wx[j].astype(BF16), row(lru_bx[j]),
        row(lru_lambda[j]), lru_w_out[j].astype(BF16))
    gf = row(norm_final)

    steps = bp * (seq // ROW_TILE)
    heads = lambda v: v.reshape(bs, XA_HEADS, XA_HD)
    ffn_args = lambda i: (row(norm_ffn[i]), ffn_w_up_b[i], ffn_conv_w[i], row(ffn_conv_b[i]), ffn_w_down_b[i], gf)

    mem_k, mem_v = _mem_kv(mem_prompt.reshape(bp * mem_len, d), norm_mem.reshape(DEPTH, 1, d), xa_w_kv_b, ROW_TILE)
    mem_k = mem_k.reshape(DEPTH, bp, mem_len, d)
    mem_v = mem_v.reshape(DEPTH, bp, mem_len, d)
    cos_p, sin_p = _rope_tables(jnp.arange(seq, dtype=F32))
    cos_s, sin_s = _rope_tables(jnp.full((bs,), PAST_LEN, F32))
    xp = x_prompt.reshape(bp * seq, d)
    xs = x_sample.reshape(bs, d)

    qkvg = _ret_proj(xp, row(norm_mix[0]), ret_w_in_b, cos_p, sin_p, ROW_TILE, BF16)
    qkvg_s = _ret_proj(xs, row(norm_mix[0]), ret_w_in_b, cos_s, sin_s, bs, F32).reshape(steps, bs // steps, RET_COLS)
    xp, ret_p = _ret_scan(qkvg, xp, ret_w_out_b, bp, seq, ROW_TILE)
    xp, ret_s, y_s = _xattn(xp, row(norm_xa[0]), xa_w_q_b[0], mem_k, mem_v, xa_w_o_b[0], 0, bp, seq, ROW_TILE,
                            ret_step=(qkvg_s, state_ret))
    xs = _matmul_res(y_s.reshape(bs, RET_V), ret_w_out_b, xs)
    q_s = heads(_norm_matmul(xs, row(norm_xa[0]), xa_w_q_b[0]))
    xp, ffn_p0 = _ffn(xp, *ffn_args(0), bp, seq, ROW_TILE, False)

    xp, lru_h_p, lru_conv_p = _lru(xp, lru_weights(0), bp, seq, ROW_TILE)
    xp, o_s = _xattn(xp, row(norm_xa[1]), xa_w_q_b[1], mem_k, mem_v, xa_w_o_b[1], 1, bp, seq, ROW_TILE,
                     mem_step=(q_s, cache_mem_k, cache_mem_v, 0))
    xp, ffn_p1 = _ffn(xp, *ffn_args(1), bp, seq, ROW_TILE, True)
    xs = _matmul_res(o_s.reshape(bs, d), xa_w_o_b[0], xs)
    xs, ffn_s0 = _ffn_step(xs, *ffn_args(0), state_ffn_conv[0].reshape(bs, -1), False)
    xs, lru_h_s, lru_conv_s = _lru_step(xs, lru_weights(0), state_lru_h[0], state_lru_conv[0].reshape(bs, -1))
    q_s = heads(_norm_matmul(xs, row(norm_xa[1]), xa_w_q_b[1]))
    xs = _matmul_res(_xattn_step(q_s, cache_mem_k, cache_mem_v, 1).reshape(bs, d), xa_w_o_b[1], xs)
    xs, ffn_s1 = _ffn_step(xs, *ffn_args(1), state_ffn_conv[1].reshape(bs, -1), True)

    mem_shape = (DEPTH, bp, mem_len, XA_HEADS, XA_HD)
    return (
        xp.reshape(bp, seq, d),
        xs.reshape(bs, 1, d),
        ret_p[None],
        ret_s,
        lru_h_p.reshape(1, bp, LRU_WIDTH),
        lru_h_s[None],
        lru_conv_p[None],
        lru_conv_s.reshape(1, bs, LRU_CONV - 1, LRU_WIDTH),
        jnp.stack([ffn_p0, ffn_p1]),
        jnp.stack([ffn_s0, ffn_s1]).reshape(DEPTH, bs, FFN_CONV - 1, FFN_DIM),
        mem_k.reshape(mem_shape),
        mem_v.reshape(mem_shape),
    )
```

```python
import functools
import math

import jax
import jax.numpy as jnp
from jax import lax
from jax.experimental import pallas as pl
from jax.experimental.pallas import tpu as pltpu

F32 = jnp.float32
BF16 = jnp.bfloat16

D_MODEL = 1024
DEPTH = 2
PAST_LEN = 16384
RET_HEADS = 4
RET_DK = D_MODEL // RET_HEADS
RET_DV = 2 * D_MODEL // RET_HEADS
ROPE_BASE = 10000.0
ROPE_HALF = RET_DK // 2
LRU_WIDTH = D_MODEL
LRU_BLOCKS = 4
LRU_BW = LRU_WIDTH // LRU_BLOCKS
LRU_CONV = 4
LRU_C = 8.0
XA_HEADS = 4
XA_HD = D_MODEL // XA_HEADS
FFN_DIM = 3 * D_MODEL
FFN_CONV = 3
EPS = 1e-6

RET_QK = RET_HEADS * RET_DK
RET_V = RET_HEADS * RET_DV
RET_COLS = 2 * RET_QK + 2 * RET_V
RET_GAMMA = tuple(1.0 - 2.0 ** (-5.0 - h) for h in range(RET_HEADS))
RET_LOG_G = tuple(math.log(g) for g in RET_GAMMA)
K_SCALE = RET_DK ** -0.5
XA_SCALE = XA_HD ** -0.5
SQRT_2_OVER_PI = math.sqrt(2.0 / math.pi)

SUBLANES = 8
VMEM_LIMIT = 56 << 20

ROW_TILE = 512
WIDE_ROW_TILE = 1024
SCAN_CHUNK = 256
FFN_COLS = 512
STEP_ROWS = 8


def _params(*sem):
    return pltpu.CompilerParams(dimension_semantics=sem, vmem_limit_bytes=VMEM_LIMIT)


def _resident(shape):
    zeros = (0,) * len(shape)
    return pl.BlockSpec(shape, lambda *_: zeros, pipeline_mode=pl.Buffered(1))


def _dot(a, b):
    return jnp.dot(a, b, preferred_element_type=F32)


def _dot_nt(a, b):
    return lax.dot_general(a, b, (((1,), (1,)), ((), ())), preferred_element_type=F32)


def _rms(x, g):
    return x * lax.rsqrt(jnp.mean(x * x, axis=-1, keepdims=True) + EPS) * g


def _gelu(x):
    return x * (0.5 * (1.0 + jnp.tanh(SQRT_2_OVER_PI * (x + 0.044715 * (x * x * x)))))


def _silu(x):
    return x * jax.nn.sigmoid(x)


def _log_sigmoid(x):
    return jnp.minimum(x, 0.0) - jnp.log(1.0 + jnp.exp(-jnp.abs(x)))


def _shift_rows(x, prev, s):
    rolled = pltpu.roll(x, s, 0)
    rows = lax.broadcasted_iota(jnp.int32, (SUBLANES, 1), 0)
    head = jnp.where(rows < s, pltpu.roll(prev, s, 0), rolled[:SUBLANES])
    return jnp.concatenate([head, rolled[SUBLANES:]], axis=0)


def _ret_proj_kernel(x_ref, g_ref, w_ref, cos_ref, sin_ref, o_ref):
    hn = _rms(x_ref[...], g_ref[...]).astype(BF16)
    cos = cos_ref[...]
    sin = sin_ref[...]
    for n in range(2 * RET_HEADS):
        c0 = n * RET_DK
        acc = _dot(hn, w_ref[:, c0:c0 + RET_DK])
        x1 = acc[:, :ROPE_HALF]
        x2 = acc[:, ROPE_HALF:]
        r1 = x1 * cos - x2 * sin
        r2 = x1 * sin + x2 * cos
        if n >= RET_HEADS:
            r1 = r1 * K_SCALE
            r2 = r2 * K_SCALE
        o_ref[:, c0:c0 + ROPE_HALF] = r1.astype(o_ref.dtype)
        o_ref[:, c0 + ROPE_HALF:c0 + RET_DK] = r2.astype(o_ref.dtype)
    for n in range(2 * RET_V // RET_DV):
        c0 = 2 * RET_QK + n * RET_DV
        o_ref[:, c0:c0 + RET_DV] = _dot(hn, w_ref[:, c0:c0 + RET_DV]).astype(o_ref.dtype)


def _ret_proj(x, g, w, cos, sin, tm, out_dtype):
    t = x.shape[0]
    period = cos.shape[0] // tm
    return pl.pallas_call(
        _ret_proj_kernel,
        grid=(t // tm,),
        in_specs=[
            pl.BlockSpec((tm, D_MODEL), lambda m: (m, 0)),
            _resident((1, D_MODEL)),
            _resident((D_MODEL, RET_COLS)),
            pl.BlockSpec((tm, ROPE_HALF), lambda m: (m % period, 0)),
            pl.BlockSpec((tm, ROPE_HALF), lambda m: (m % period, 0)),
        ],
        out_specs=pl.BlockSpec((tm, RET_COLS), lambda m: (m, 0)),
        out_shape=jax.ShapeDtypeStruct((t, RET_COLS), out_dtype),
        compiler_params=_params("arbitrary"),
        name="ret_proj",
    )(x, g, w, cos, sin)


def _ret_scan_kernel(q_ref, k_ref, v_ref, g_ref, x_ref, w_ref, xo_ref, s_ref, y_scr, intra_scr):
    c = SCAN_CHUNK

    @pl.when((pl.program_id(0) == 0) & (pl.program_id(1) == 0))
    def _():
        row = lax.broadcasted_iota(jnp.int32, (c, c), 0).astype(F32)
        col = lax.broadcasted_iota(jnp.int32, (c, c), 1).astype(F32)
        rel = row - col
        for h in range(RET_HEADS):
            intra_scr[h] = jnp.where(rel >= 0, jnp.exp(RET_LOG_G[h] * jnp.maximum(rel, 0.0)), 0.0)

    @pl.when(pl.program_id(1) == 0)
    def _():
        s_ref[...] = jnp.zeros_like(s_ref)

    idx = lax.broadcasted_iota(jnp.int32, (c, 1), 0).astype(F32)
    for h in range(RET_HEADS):
        lg = RET_LOG_G[h]
        intra = intra_scr[h]
        q_dec = jnp.exp(lg * (idx + 1.0))
        k_dec = jnp.exp(lg * (c - 1.0 - idx))
        chunk_dec = math.exp(lg * c)
        qk_cols = slice(h * RET_DK, (h + 1) * RET_DK)
        v_cols = slice(h * RET_DV, (h + 1) * RET_DV)
        for ci in range(q_ref.shape[0] // c):
            rows = slice(ci * c, (ci + 1) * c)
            qc = q_ref[rows, qk_cols]
            kc = k_ref[rows, qk_cols]
            vc = v_ref[rows, v_cols]
            s = s_ref[h]
            att = _dot_nt(qc, kc) * intra
            o = _dot(att.astype(BF16), vc) + _dot((qc.astype(F32) * q_dec).astype(BF16), s.astype(BF16))
            kd_t = (kc.astype(F32) * k_dec).T.astype(BF16)
            s_ref[h] = s * chunk_dec + _dot(kd_t, vc)
            o = o * lax.rsqrt(jnp.mean(o * o, axis=-1, keepdims=True) + EPS)
            y_scr[rows, v_cols] = (_silu(g_ref[rows, v_cols].astype(F32)) * o).astype(BF16)
    xo_ref[...] = x_ref[...] + _dot(y_scr[...], w_ref[...])


def _ret_scan(qkvg, x, w_out, batch, seq, tm):
    t = x.shape[0]
    nj = seq // tm
    return pl.pallas_call(
        _ret_scan_kernel,
        grid=(batch, nj),
        in_specs=[
            pl.BlockSpec((tm, RET_QK), lambda b, j: (b * nj + j, 0)),
            pl.BlockSpec((tm, RET_QK), lambda b, j: (b * nj + j, 1)),
            pl.BlockSpec((tm, RET_V), lambda b, j: (b * nj + j, 1)),
            pl.BlockSpec((tm, RET_V), lambda b, j: (b * nj + j, 2)),
            pl.BlockSpec((tm, D_MODEL), lambda b, j: (b * nj + j, 0)),
            _resident((RET_V, D_MODEL)),
        ],
        out_specs=[
            pl.BlockSpec((tm, D_MODEL), lambda b, j: (b * nj + j, 0)),
            pl.BlockSpec((None, RET_HEADS, RET_DK, RET_DV), lambda b, j: (b, 0, 0, 0)),
        ],
        out_shape=[
            jax.ShapeDtypeStruct((t, D_MODEL), F32),
            jax.ShapeDtypeStruct((batch, RET_HEADS, RET_DK, RET_DV), F32),
        ],
        scratch_shapes=[
            pltpu.VMEM((tm, RET_V), BF16),
            pltpu.VMEM((RET_HEADS, SCAN_CHUNK, SCAN_CHUNK), F32),
        ],
        compiler_params=_params("arbitrary", "arbitrary"),
        name="ret_scan",
    )(qkvg, qkvg, qkvg, qkvg, x, w_out)


def _ret_rows(q_ref, k_ref, v_ref, g_ref, s_ref, so_ref, y_ref):
    n = q_ref.shape[0]
    pad = jnp.zeros((SUBLANES - n, RET_DK), F32)
    v = v_ref[...]
    gate = _silu(g_ref[...])
    for h in range(RET_HEADS):
        qk_cols = slice(h * RET_DK, (h + 1) * RET_DK)
        v_cols = slice(h * RET_DV, (h + 1) * RET_DV)
        q_t = jnp.concatenate([q_ref[:, qk_cols], pad], axis=0).T
        k_t = jnp.concatenate([k_ref[:, qk_cols], pad], axis=0).T
        for i in range(n):
            s_new = s_ref[i, h] * RET_GAMMA[h] + k_t[:, i:i + 1] * v[i:i + 1, v_cols]
            so_ref[i, h] = s_new
            o = jnp.sum(q_t[:, i:i + 1] * s_new, axis=0, keepdims=True)
            o = o * lax.rsqrt(jnp.mean(o * o, axis=-1, keepdims=True) + EPS)
            y_ref[i:i + 1, v_cols] = gate[i:i + 1, v_cols] * o


def _ret_rows_specs(rows, step):
    state = pl.BlockSpec((None, rows, RET_HEADS, RET_DK, RET_DV), lambda *ids: (0, step(*ids), 0, 0, 0))
    ins = [
        pl.BlockSpec((None, rows, RET_QK), lambda *ids: (step(*ids), 0, 0)),
        pl.BlockSpec((None, rows, RET_QK), lambda *ids: (step(*ids), 0, 1)),
        pl.BlockSpec((None, rows, RET_V), lambda *ids: (step(*ids), 0, 1)),
        pl.BlockSpec((None, rows, RET_V), lambda *ids: (step(*ids), 0, 2)),
        state,
    ]
    outs = [state, pl.BlockSpec((None, rows, RET_V), lambda *ids: (step(*ids), 0, 0))]
    return ins, outs


def _norm_matmul_kernel(x_ref, g_ref, w_ref, o_ref):
    o_ref[...] = _dot(_rms(x_ref[...], g_ref[...]).astype(BF16), w_ref[...])


def _norm_matmul(x, g, w):
    return pl.pallas_call(
        _norm_matmul_kernel,
        out_shape=jax.ShapeDtypeStruct((x.shape[0], w.shape[1]), F32),
        compiler_params=pltpu.CompilerParams(vmem_limit_bytes=VMEM_LIMIT),
        name="norm_matmul",
    )(x, g, w)


def _matmul_res_kernel(y_ref, w_ref, x_ref, o_ref):
    o_ref[...] = x_ref[...] + _dot(y_ref[...].astype(BF16), w_ref[...])


def _matmul_res(y, w, x):
    return pl.pallas_call(
        _matmul_res_kernel,
        out_shape=jax.ShapeDtypeStruct(x.shape, F32),
        compiler_params=pltpu.CompilerParams(vmem_limit_bytes=VMEM_LIMIT),
        name="matmul_res",
    )(y, w, x)


def _mem_kv_kernel(m_ref, g_ref, w_ref, k_ref, v_ref):
    hn = _rms(m_ref[...], g_ref[...]).astype(BF16)
    k_ref[...] = _dot(hn, w_ref[:, :D_MODEL])
    v_ref[...] = _dot(hn, w_ref[:, D_MODEL:])


def _mem_kv(mem, g, w, tm):
    t = mem.shape[0]
    out = jax.ShapeDtypeStruct((DEPTH, t, D_MODEL), F32)
    return pl.pallas_call(
        _mem_kv_kernel,
        grid=(DEPTH, t // tm),
        in_specs=[
            pl.BlockSpec((tm, D_MODEL), lambda i, m: (m, 0)),
            pl.BlockSpec((None, 1, D_MODEL), lambda i, m: (i, 0, 0)),
            pl.BlockSpec((None, D_MODEL, 2 * D_MODEL), lambda i, m: (i, 0, 0)),
        ],
        out_specs=[
            pl.BlockSpec((None, tm, D_MODEL), lambda i, m: (i, m, 0)),
            pl.BlockSpec((None, tm, D_MODEL), lambda i, m: (i, m, 0)),
        ],
        out_shape=[out, out],
        compiler_params=_params("arbitrary", "arbitrary"),
        name="mem_kv",
    )(mem, g, w)


def _xattn_rows_specs(rows, mem_len, layer, step):
    row = pl.BlockSpec((rows, XA_HEADS, XA_HD), lambda *ids: (step(*ids), 0, 0))
    mem = pl.BlockSpec((None, rows, mem_len, XA_HEADS, XA_HD), lambda *ids: (layer, step(*ids), 0, 0, 0))
    return [row, mem, mem], row


def _xattn_row(q_ref, mk_ref, mv_ref, o_ref, i):
    s = jnp.sum(mk_ref[i] * q_ref[i][None], axis=-1, keepdims=True) * XA_SCALE
    e = jnp.exp(s - jnp.max(s, axis=0, keepdims=True))
    p = e / jnp.sum(e, axis=0, keepdims=True)
    o_ref[i] = jnp.sum(p * mv_ref[i], axis=0)


def _xattn_step_kernel(q_ref, mk_ref, mv_ref, o_ref):
    for i in range(q_ref.shape[0]):
        _xattn_row(q_ref, mk_ref, mv_ref, o_ref, i)


def _xattn_step(q, mk, mv, layer):
    ins, out = _xattn_rows_specs(STEP_ROWS, mk.shape[2], layer, lambda i: i)
    return pl.pallas_call(
        _xattn_step_kernel,
        grid=(q.shape[0] // STEP_ROWS,),
        in_specs=ins,
        out_specs=out,
        out_shape=jax.ShapeDtypeStruct(q.shape, F32),
        compiler_params=_params("arbitrary"),
        name="xattn_step",
    )(q, mk, mv)


def _xattn_main(x_ref, g_ref, wq_ref, mk_ref, mv_ref, wo_ref, xo_ref, o_scr, per_head=None):
    x = x_ref[...]
    q = _dot(_rms(x, g_ref[...]).astype(BF16), wq_ref[...])
    for h in range(XA_HEADS):
        cols = slice(h * XA_HD, (h + 1) * XA_HD)
        s = _dot_nt(q[:, cols].astype(BF16), mk_ref[:, cols].astype(BF16)) * XA_SCALE
        e = jnp.exp(s - jnp.max(s, axis=-1, keepdims=True))
        p = e / jnp.sum(e, axis=-1, keepdims=True)
        o_scr[:, cols] = _dot(p.astype(BF16), mv_ref[:, cols].astype(BF16)).astype(BF16)
        if per_head is not None:
            per_head(h)
    xo_ref[...] = x + _dot(o_scr[...], wo_ref[...])


def _xattn_kernel(x_ref, g_ref, wq_ref, mk_ref, mv_ref, wo_ref, xo_ref, o_scr):
    _xattn_main(x_ref, g_ref, wq_ref, mk_ref, mv_ref, wo_ref, xo_ref, o_scr)


def _xattn_mem_kernel(x_ref, g_ref, wq_ref, mk_ref, mv_ref, wo_ref, qs_ref, ck_ref, cv_ref,
                      xo_ref, os_ref, o_scr):
    rows = qs_ref.shape[0]
    assert rows <= XA_HEADS

    def per_head(h):
        if h < rows:
            _xattn_row(qs_ref, ck_ref, cv_ref, os_ref, h)

    _xattn_main(x_ref, g_ref, wq_ref, mk_ref, mv_ref, wo_ref, xo_ref, o_scr, per_head)


def _xattn(x, g, wq, mk, mv, wo, layer, batch, seq, tm, mem_step=None):
    t = x.shape[0]
    nj = seq // tm
    mem_len = mk.shape[2]
    step = lambda b, j: b * nj + j
    mem_spec = pl.BlockSpec((None, None, mem_len, D_MODEL), lambda b, j: (layer, b, 0, 0))
    row_spec = pl.BlockSpec((tm, D_MODEL), lambda b, j: (step(b, j), 0))
    in_specs = [row_spec, _resident((1, D_MODEL)), _resident((D_MODEL, D_MODEL)), mem_spec, mem_spec,
                _resident((D_MODEL, D_MODEL))]
    out_specs = [row_spec]
    out_shape = [jax.ShapeDtypeStruct((t, D_MODEL), F32)]
    args = [x, g, wq, mk, mv, wo]
    body = _xattn_kernel
    if mem_step is not None:
        q_s, cache_k, cache_v, cache_layer = mem_step
        rows = q_s.shape[0] // (batch * nj)
        assert rows * batch * nj == q_s.shape[0]
        side_in, side_out = _xattn_rows_specs(rows, cache_k.shape[2], cache_layer, step)
        in_specs += side_in
        out_specs += [side_out]
        out_shape += [jax.ShapeDtypeStruct(q_s.shape, F32)]
        args += [q_s, cache_k, cache_v]
        body = _xattn_mem_kernel
    return pl.pallas_call(
        body,
        grid=(batch, nj),
        in_specs=in_specs,
        out_specs=out_specs,
        out_shape=out_shape,
        scratch_shapes=[pltpu.VMEM((tm, D_MODEL), BF16)],
        compiler_params=_params("arbitrary", "arbitrary"),
        name="xattn",
    )(*args)


def _ffn_tail(x, y_scr, wdn_ref, gf_ref, xo_ref, final):
    out = x + _dot(y_scr[...], wdn_ref[...])
    if final:
        out = _rms(out, gf_ref[...])
    xo_ref[...] = out


def _ffn_kernel(x_ref, g_ref, wup_ref, cw_ref, cb_ref, wdn_ref, gf_ref, xo_ref, buf_ref,
                carry_scr, y_scr, *, final):
    @pl.when(pl.program_id(1) == 0)
    def _():
        carry_scr[...] = jnp.zeros_like(carry_scr)

    x = x_ref[...]
    tm = x.shape[0]
    hn = _rms(x, g_ref[...]).astype(BF16)
    for n in range(FFN_DIM // FFN_COLS):
        cols = slice(n * FFN_COLS, (n + 1) * FFN_COLS)
        gcols = slice(FFN_DIM + n * FFN_COLS, FFN_DIM + (n + 1) * FFN_COLS)
        u = _dot(hn, wup_ref[:, cols])
        gate = _dot(hn, wup_ref[:, gcols])
        prev = carry_scr[:, cols]
        uc = cb_ref[:, cols] + _shift_rows(u, prev, 2) * cw_ref[0:1, cols]
        uc = uc + _shift_rows(u, prev, 1) * cw_ref[1:2, cols]
        uc = uc + u * cw_ref[2:3, cols]
        y_scr[:, cols] = (_gelu(uc) * gate).astype(BF16)
        carry_scr[:, cols] = u[tm - SUBLANES:, :]
        buf_ref[:, cols] = u[tm - (FFN_CONV - 1):, :]
    _ffn_tail(x, y_scr, wdn_ref, gf_ref, xo_ref, final)


def _ffn(x, g, wup, cw, cb, wdn, gf, batch, seq, tm, final):
    t = x.shape[0]
    nj = seq // tm
    step = lambda b, j: b * nj + j
    return pl.pallas_call(
        functools.partial(_ffn_kernel, final=final),
        grid=(batch, nj),
        in_specs=[
            pl.BlockSpec((tm, D_MODEL), lambda b, j: (step(b, j), 0)),
            _resident((1, D_MODEL)),
            _resident((D_MODEL, 2 * FFN_DIM)),
            _resident((FFN_CONV, FFN_DIM)),
            _resident((1, FFN_DIM)),
            _resident((FFN_DIM, D_MODEL)),
            _resident((1, D_MODEL)),
        ],
        out_specs=[
            pl.BlockSpec((tm, D_MODEL), lambda b, j: (step(b, j), 0)),
            pl.BlockSpec((None, FFN_CONV - 1, FFN_DIM), lambda b, j: (b, 0, 0)),
        ],
        out_shape=[
            jax.ShapeDtypeStruct((t, D_MODEL), F32),
            jax.ShapeDtypeStruct((batch, FFN_CONV - 1, FFN_DIM), F32),
        ],
        scratch_shapes=[pltpu.VMEM((SUBLANES, FFN_DIM), F32), pltpu.VMEM((tm, FFN_DIM), BF16)],
        compiler_params=_params("arbitrary", "arbitrary"),
        name="ffn",
    )(x, g, wup, cw, cb, wdn, gf)


def _ffn_step_kernel(x_ref, g_ref, wup_ref, cw_ref, cb_ref, wdn_ref, gf_ref, buf_ref, xo_ref, nbuf_ref,
                     y_scr, *, final):
    x = x_ref[...]
    hn = _rms(x, g_ref[...]).astype(BF16)
    for n in range(FFN_DIM // FFN_COLS):
        cols = slice(n * FFN_COLS, (n + 1) * FFN_COLS)
        gcols = slice(FFN_DIM + n * FFN_COLS, FFN_DIM + (n + 1) * FFN_COLS)
        u = _dot(hn, wup_ref[:, cols])
        gate = _dot(hn, wup_ref[:, gcols])
        b0 = buf_ref[:, cols]
        b1 = buf_ref[:, gcols]
        uc = cb_ref[:, cols] + b0 * cw_ref[0:1, cols]
        uc = uc + b1 * cw_ref[1:2, cols]
        uc = uc + u * cw_ref[2:3, cols]
        y_scr[:, cols] = (_gelu(uc) * gate).astype(BF16)
        nbuf_ref[:, cols] = b1
        nbuf_ref[:, gcols] = u
    _ffn_tail(x, y_scr, wdn_ref, gf_ref, xo_ref, final)


def _ffn_step(x, g, wup, cw, cb, wdn, gf, buf, final):
    t = x.shape[0]
    return pl.pallas_call(
        functools.partial(_ffn_step_kernel, final=final),
        out_shape=[
            jax.ShapeDtypeStruct((t, D_MODEL), F32),
            jax.ShapeDtypeStruct(buf.shape, F32),
        ],
        scratch_shapes=[pltpu.VMEM((t, FFN_DIM), BF16)],
        compiler_params=pltpu.CompilerParams(vmem_limit_bytes=VMEM_LIMIT),
        name="ffn_step",
    )(x, g, wup, cw, cb, wdn, gf, buf)


def _lru_gates(xc, wa_ref, ba_ref, wx_ref, bx_ref, lam_ref, a_out, u_out):
    xcb = xc.astype(BF16)
    log_s = _log_sigmoid(lam_ref[...])
    for blk in range(LRU_BLOCKS):
        cols = slice(blk * LRU_BW, (blk + 1) * LRU_BW)
        r = jax.nn.sigmoid(_dot(xcb[:, cols], wa_ref[blk]) + ba_ref[:, cols])
        i = jax.nn.sigmoid(_dot(xcb[:, cols], wx_ref[blk]) + bx_ref[:, cols])
        log_a = LRU_C * r * log_s[:, cols]
        a = jnp.exp(log_a)
        a_out[:, cols] = a
        u_out[:, cols] = jnp.sqrt(-jnp.tanh(log_a) * (a * a + 1.0)) * (i * xc[:, cols])


def _lru_kernel(x_ref, g_ref, win_ref, cw_ref, cb_ref, wa_ref, ba_ref, wx_ref, bx_ref, lam_ref, wout_ref,
                q_ref, k_ref, v_ref, gt_ref, s_ref, xo_ref, h_ref, buf_ref, so_ref, y_ref,
                carry_scr, h_scr, a_scr, u_scr):
    @pl.when(pl.program_id(1) == 0)
    def _():
        carry_scr[...] = jnp.zeros_like(carry_scr)
        h_scr[...] = jnp.zeros_like(h_scr)

    x = x_ref[...]
    tm = x.shape[0]
    hn = _rms(x, g_ref[...]).astype(BF16)
    xb = _dot(hn, win_ref[:, :LRU_WIDTH])
    prev = carry_scr[...]
    xc = cb_ref[...] + _shift_rows(xb, prev, 3) * cw_ref[0:1, :]
    xc = xc + _shift_rows(xb, prev, 2) * cw_ref[1:2, :]
    xc = xc + _shift_rows(xb, prev, 1) * cw_ref[2:3, :]
    xc = xc + xb * cw_ref[3:4, :]
    carry_scr[...] = xb[tm - SUBLANES:, :]
    buf_ref[...] = xb[tm - (LRU_CONV - 1):, :]
    _lru_gates(xc, wa_ref, ba_ref, wx_ref, bx_ref, lam_ref, a_scr, u_scr)

    sub = lax.broadcasted_iota(jnp.int32, (SUBLANES, LRU_WIDTH), 0)

    h = h_scr[...]
    for gi in range(tm // SUBLANES):
        rows = slice(gi * SUBLANES, (gi + 1) * SUBLANES)
        a = a_scr[rows, :]
        u = u_scr[rows, :]
        for s in (1, 2, 4):
            keep = sub >= s
            u = jnp.where(keep, a * pltpu.roll(u, s, 0) + u, u)
            a = jnp.where(keep, a * pltpu.roll(a, s, 0), a)
        hs = a * h + u
        u_scr[rows, :] = hs
        h = hs[SUBLANES - 1:, :]
    h_scr[...] = h
    h_ref[...] = h
    gb = _dot(hn, win_ref[:, LRU_WIDTH:])
    y = (_gelu(gb) * u_scr[...]).astype(BF16)
    xo_ref[...] = x + _dot(y, wout_ref[...])
    _ret_rows(q_ref, k_ref, v_ref, gt_ref, s_ref, so_ref, y_ref)


def _lru_weight_specs():
    return [
        _resident((1, D_MODEL)),
        _resident((D_MODEL, 2 * LRU_WIDTH)),
        _resident((LRU_CONV, LRU_WIDTH)),
        _resident((1, LRU_WIDTH)),
        _resident((LRU_BLOCKS, LRU_BW, LRU_BW)),
        _resident((1, LRU_WIDTH)),
        _resident((LRU_BLOCKS, LRU_BW, LRU_BW)),
        _resident((1, LRU_WIDTH)),
        _resident((1, LRU_WIDTH)),
        _resident((LRU_WIDTH, D_MODEL)),
    ]


def _lru(x, weights, qkvg_s, state, batch, seq, tm):
    t = x.shape[0]
    nj = seq // tm
    steps, rows, _ = qkvg_s.shape
    assert steps == batch * nj and steps * rows == state.shape[1]
    ret_in, ret_out = _ret_rows_specs(rows, lambda b, j: b * nj + j)
    return pl.pallas_call(
        _lru_kernel,
        grid=(batch, nj),
        in_specs=[pl.BlockSpec((tm, D_MODEL), lambda b, j: (b * nj + j, 0))] + _lru_weight_specs() + ret_in,
        out_specs=[
            pl.BlockSpec((tm, D_MODEL), lambda b, j: (b * nj + j, 0)),
            pl.BlockSpec((None, 1, LRU_WIDTH), lambda b, j: (b, 0, 0)),
            pl.BlockSpec((None, LRU_CONV - 1, LRU_WIDTH), lambda b, j: (b, 0, 0)),
        ] + ret_out,
        out_shape=[
            jax.ShapeDtypeStruct((t, D_MODEL), F32),
            jax.ShapeDtypeStruct((batch, 1, LRU_WIDTH), F32),
            jax.ShapeDtypeStruct((batch, LRU_CONV - 1, LRU_WIDTH), F32),
            jax.ShapeDtypeStruct(state.shape, F32),
            jax.ShapeDtypeStruct((steps, rows, RET_V), F32),
        ],
        scratch_shapes=[
            pltpu.VMEM((SUBLANES, LRU_WIDTH), F32),
            pltpu.VMEM((1, LRU_WIDTH), F32),
            pltpu.VMEM((tm, LRU_WIDTH), F32),
            pltpu.VMEM((tm, LRU_WIDTH), F32),
        ],
        compiler_params=_params("arbitrary", "arbitrary"),
        name="lru",
    )(x, *weights, qkvg_s, qkvg_s, qkvg_s, qkvg_s, state)


def _lru_step_kernel(x_ref, g_ref, win_ref, cw_ref, cb_ref, wa_ref, ba_ref, wx_ref, bx_ref, lam_ref, wout_ref,
                     h0_ref, buf_ref, xo_ref, h_ref, nbuf_ref, a_scr, u_scr):
    x = x_ref[...]
    w = LRU_WIDTH
    hn = _rms(x, g_ref[...]).astype(BF16)
    xb = _dot(hn, win_ref[:, :w])
    xc = cb_ref[...] + buf_ref[:, 0:w] * cw_ref[0:1, :]
    xc = xc + buf_ref[:, w:2 * w] * cw_ref[1:2, :]
    xc = xc + buf_ref[:, 2 * w:3 * w] * cw_ref[2:3, :]
    xc = xc + xb * cw_ref[3:4, :]
    nbuf_ref[:, 0:2 * w] = buf_ref[:, w:3 * w]
    nbuf_ref[:, 2 * w:3 * w] = xb
    _lru_gates(xc, wa_ref, ba_ref, wx_ref, bx_ref, lam_ref, a_scr, u_scr)
    hs = a_scr[...] * h0_ref[...] + u_scr[...]
    h_ref[...] = hs
    gb = _dot(hn, win_ref[:, w:])
    xo_ref[...] = x + _dot((_gelu(gb) * hs).astype(BF16), wout_ref[...])


def _lru_step(x, weights, h0, buf):
    t = x.shape[0]
    return pl.pallas_call(
        _lru_step_kernel,
        out_shape=[
            jax.ShapeDtypeStruct((t, D_MODEL), F32),
            jax.ShapeDtypeStruct(h0.shape, F32),
            jax.ShapeDtypeStruct(buf.shape, F32),
        ],
        scratch_shapes=[pltpu.VMEM((t, LRU_WIDTH), F32), pltpu.VMEM((t, LRU_WIDTH), F32)],
        compiler_params=pltpu.CompilerParams(vmem_limit_bytes=VMEM_LIMIT),
        name="lru_step",
    )(x, *weights, h0, buf)


def _rope_tables(positions):
    inv = ROPE_BASE ** (-jnp.arange(ROPE_HALF, dtype=F32) / ROPE_HALF)
    ang = positions[:, None] * inv[None, :]
    return jnp.cos(ang), jnp.sin(ang)


def kernel(x_prompt, x_sample, state_ret, state_lru_h, state_lru_conv, state_ffn_conv, cache_mem_k, cache_mem_v, mem_prompt, norm_mix, norm_xa, norm_mem, norm_ffn, norm_final, ret_w_in, ret_w_out, lru_w_in, lru_conv_w, lru_conv_b, lru_wa, lru_ba, lru_wx, lru_bx, lru_lambda, lru_w_out, xa_w_q, xa_w_kv, xa_w_o, ffn_w_up, ffn_conv_w, ffn_conv_b, ffn_w_down):
    bp, seq, d = x_prompt.shape
    bs = x_sample.shape[0]
    mem_len = mem_prompt.shape[1]
    assert d == D_MODEL and x_sample.shape[1] == 1
    assert seq % WIDE_ROW_TILE == 0 and WIDE_ROW_TILE % ROW_TILE == 0
    assert ROW_TILE % SCAN_CHUNK == 0 and bs % STEP_ROWS == 0

    row = lambda v: v.reshape(1, -1)
    ret_w_in_b = ret_w_in[0].astype(BF16)
    ret_w_out_b = ret_w_out[0].astype(BF16)
    xa_w_q_b = xa_w_q.astype(BF16)
    xa_w_kv_b = xa_w_kv.astype(BF16)
    xa_w_o_b = xa_w_o.astype(BF16)
    ffn_w_up_b = ffn_w_up.astype(BF16)
    ffn_w_down_b = ffn_w_down.astype(BF16)
    lru_weights = lambda j: (
        row(norm_mix[1]), lru_w_in[j].astype(BF16), lru_conv_w[j], row(lru_conv_b[j]),
        lru_wa[j].astype(BF16), row(lru_ba[j]), lru_wx[j].astype(BF16), row(lru_bx[j]),
        row(lru_lambda[j]), lru_w_out[j].astype(BF16))
    gf = row(norm_final)

    steps = bp * (seq // ROW_TILE)
    heads = lambda v: v.reshape(bs, XA_HEADS, XA_HD)
    ffn_args = lambda i: (row(norm_ffn[i]), ffn_w_up_b[i], ffn_conv_w[i], row(ffn_conv_b[i]), ffn_w_down_b[i], gf)

    mem_k, mem_v = _mem_kv(mem_prompt.reshape(bp * mem_len, d), norm_mem.reshape(DEPTH, 1, d), xa_w_kv_b, ROW_TILE)
    mem_k = mem_k.reshape(DEPTH, bp, mem_len, d)
    mem_v = mem_v.reshape(DEPTH, bp, mem_len, d)
    cos_p, sin_p = _rope_tables(jnp.arange(seq, dtype=F32))
    cos_s, sin_s = _rope_tables(jnp.full((bs,), PAST_LEN, F32))
    xp = x_prompt.reshape(bp * seq, d)
    xs = x_sample.reshape(bs, d)

    qkvg = _ret_proj(xp, row(norm_mix[0]), ret_w_in_b, cos_p, sin_p, WIDE_ROW_TILE, BF16)
    qkvg_s = _ret_proj(xs, row(norm_mix[0]), ret_w_in_b, cos_s, sin_s, bs, F32).reshape(steps, bs // steps, RET_COLS)
    xp, ret_p = _ret_scan(qkvg, xp, ret_w_out_b, bp, seq, ROW_TILE)
    xp, = _xattn(xp, row(norm_xa[0]), xa_w_q_b[0], mem_k, mem_v, xa_w_o_b[0], 0, bp, seq, WIDE_ROW_TILE)
    xp, ffn_p0 = _ffn(xp, *ffn_args(0), bp, seq, WIDE_ROW_TILE, False)

    xp, lru_h_p, lru_conv_p, ret_s, y_s = _lru(xp, lru_weights(0), qkvg_s, state_ret, bp, seq, ROW_TILE)
    xs = _matmul_res(y_s.reshape(bs, RET_V), ret_w_out_b, xs)
    q_s = heads(_norm_matmul(xs, row(norm_xa[0]), xa_w_q_b[0]))
    xp, o_s = _xattn(xp, row(norm_xa[1]), xa_w_q_b[1], mem_k, mem_v, xa_w_o_b[1], 1, bp, seq, ROW_TILE,
                     mem_step=(q_s, cache_mem_k, cache_mem_v, 0))
    xp, ffn_p1 = _ffn(xp, *ffn_args(1), bp, seq, WIDE_ROW_TILE, True)
    xs = _matmul_res(o_s.reshape(bs, d), xa_w_o_b[0], xs)
    xs, ffn_s0 = _ffn_step(xs, *ffn_args(0), state_ffn_conv[0].reshape(bs, -1), False)
    xs, lru_h_s, lru_conv_s = _lru_step(xs, lru_weights(0), state_lru_h[0], state_lru_conv[0].reshape(bs, -1))
    q_s = heads(_norm_matmul(xs, row(norm_xa[1]), xa_w_q_b[1]))
    xs = _matmul_res(_xattn_step(q_s, cache_mem_k, cache_mem_v, 1).reshape(bs, d), xa_w_o_b[1], xs)
    xs, ffn_s1 = _ffn_step(xs, *ffn_args(1), state_ffn_conv[1].reshape(bs, -1), True)

    mem_shape = (DEPTH, bp, mem_len, XA_HEADS, XA_HD)
    return (
        xp.reshape(bp, seq, d),
        xs.reshape(bs, 1, d),
        ret_p[None],
        ret_s,
        lru_h_p.reshape(1, bp, LRU_WIDTH),
        lru_h_s[None],
        lru_conv_p[None],
        lru_conv_s.reshape(1, bs, LRU_CONV - 1, LRU_WIDTH),
        jnp.stack([ffn_p0, ffn_p1]),
        jnp.stack([ffn_s0, ffn_s1]).reshape(DEPTH, bs, FFN_CONV - 1, FFN_DIM),
        mem_k.reshape(mem_shape),
        mem_v.reshape(mem_shape),
    )
```

```python
import functools
import math

import jax
import jax.numpy as jnp
from jax import lax
from jax.experimental import pallas as pl
from jax.experimental.pallas import tpu as pltpu

F32 = jnp.float32
BF16 = jnp.bfloat16

D_MODEL = 1024
DEPTH = 2
PAST_LEN = 16384
RET_HEADS = 4
RET_DK = D_MODEL // RET_HEADS
RET_DV = 2 * D_MODEL // RET_HEADS
ROPE_BASE = 10000.0
ROPE_HALF = RET_DK // 2
LRU_WIDTH = D_MODEL
LRU_BLOCKS = 4
LRU_BW = LRU_WIDTH // LRU_BLOCKS
LRU_CONV = 4
LRU_C = 8.0
XA_HEADS = 4
XA_HD = D_MODEL // XA_HEADS
FFN_DIM = 3 * D_MODEL
FFN_CONV = 3
EPS = 1e-6

RET_QK = RET_HEADS * RET_DK
RET_V = RET_HEADS * RET_DV
RET_COLS = 2 * RET_QK + 2 * RET_V
RET_GAMMA = tuple(1.0 - 2.0 ** (-5.0 - h) for h in range(RET_HEADS))
RET_LOG_G = tuple(math.log(g) for g in RET_GAMMA)
K_SCALE = RET_DK ** -0.5
XA_SCALE = XA_HD ** -0.5
SQRT_2_OVER_PI = math.sqrt(2.0 / math.pi)

SUBLANES = 8
VMEM_LIMIT = 56 << 20

ROW_TILE = 512
WIDE_ROW_TILE = 1024
SCAN_CHUNK = 256
FFN_COLS = 512


def _params(*sem):
    return pltpu.CompilerParams(dimension_semantics=sem, vmem_limit_bytes=VMEM_LIMIT)


def _resident(shape):
    zeros = (0,) * len(shape)
    return pl.BlockSpec(shape, lambda *_: zeros, pipeline_mode=pl.Buffered(1))


def _dot(a, b):
    return jnp.dot(a, b, preferred_element_type=F32)


def _dot_nt(a, b):
    return lax.dot_general(a, b, (((1,), (1,)), ((), ())), preferred_element_type=F32)


def _rms(x, g):
    return x * lax.rsqrt(jnp.mean(x * x, axis=-1, keepdims=True) + EPS) * g


def _gelu(x):
    return x * (0.5 * (1.0 + jnp.tanh(SQRT_2_OVER_PI * (x + 0.044715 * (x * x * x)))))


def _silu(x):
    return x * jax.nn.sigmoid(x)


def _log_sigmoid(x):
    return jnp.minimum(x, 0.0) - jnp.log(1.0 + jnp.exp(-jnp.abs(x)))


def _shift_rows(x, prev, s):
    rolled = pltpu.roll(x, s, 0)
    rows = lax.broadcasted_iota(jnp.int32, (SUBLANES, 1), 0)
    head = jnp.where(rows < s, pltpu.roll(prev, s, 0), rolled[:SUBLANES])
    return jnp.concatenate([head, rolled[SUBLANES:]], axis=0)


def _ret_proj_kernel(x_ref, g_ref, w_ref, cos_ref, sin_ref, o_ref):
    hn = _rms(x_ref[...], g_ref[...]).astype(BF16)
    cos = cos_ref[...]
    sin = sin_ref[...]
    for n in range(2 * RET_HEADS):
        c0 = n * RET_DK
        acc = _dot(hn, w_ref[:, c0:c0 + RET_DK])
        x1 = acc[:, :ROPE_HALF]
        x2 = acc[:, ROPE_HALF:]
        r1 = x1 * cos - x2 * sin
        r2 = x1 * sin + x2 * cos
        if n >= RET_HEADS:
            r1 = r1 * K_SCALE
            r2 = r2 * K_SCALE
        o_ref[:, c0:c0 + ROPE_HALF] = r1.astype(o_ref.dtype)
        o_ref[:, c0 + ROPE_HALF:c0 + RET_DK] = r2.astype(o_ref.dtype)
    for n in range(2 * RET_V // RET_DV):
        c0 = 2 * RET_QK + n * RET_DV
        o_ref[:, c0:c0 + RET_DV] = _dot(hn, w_ref[:, c0:c0 + RET_DV]).astype(o_ref.dtype)


def _ret_proj(x, g, w, cos, sin, tm, out_dtype):
    t = x.shape[0]
    period = cos.shape[0] // tm
    return pl.pallas_call(
        _ret_proj_kernel,
        grid=(t // tm,),
        in_specs=[
            pl.BlockSpec((tm, D_MODEL), lambda m: (m, 0)),
            _resident((1, D_MODEL)),
            _resident((D_MODEL, RET_COLS)),
            pl.BlockSpec((tm, ROPE_HALF), lambda m: (m % period, 0)),
            pl.BlockSpec((tm, ROPE_HALF), lambda m: (m % period, 0)),
        ],
        out_specs=pl.BlockSpec((tm, RET_COLS), lambda m: (m, 0)),
        out_shape=jax.ShapeDtypeStruct((t, RET_COLS), out_dtype),
        compiler_params=_params("arbitrary"),
        name="ret_proj",
    )(x, g, w, cos, sin)


def _ret_scan_kernel(q_ref, k_ref, v_ref, g_ref, x_ref, w_ref, xo_ref, s_ref, y_scr, intra_scr):
    c = SCAN_CHUNK

    @pl.when((pl.program_id(0) == 0) & (pl.program_id(1) == 0))
    def _():
        row = lax.broadcasted_iota(jnp.int32, (c, c), 0).astype(F32)
        col = lax.broadcasted_iota(jnp.int32, (c, c), 1).astype(F32)
        rel = row - col
        for h in range(RET_HEADS):
            intra_scr[h] = jnp.where(rel >= 0, jnp.exp(RET_LOG_G[h] * jnp.maximum(rel, 0.0)), 0.0)

    @pl.when(pl.program_id(1) == 0)
    def _():
        s_ref[...] = jnp.zeros_like(s_ref)

    idx = lax.broadcasted_iota(jnp.int32, (c, 1), 0).astype(F32)
    for h in range(RET_HEADS):
        lg = RET_LOG_G[h]
        intra = intra_scr[h]
        q_dec = jnp.exp(lg * (idx + 1.0))
        k_dec = jnp.exp(lg * (c - 1.0 - idx))
        chunk_dec = math.exp(lg * c)
        qk_cols = slice(h * RET_DK, (h + 1) * RET_DK)
        v_cols = slice(h * RET_DV, (h + 1) * RET_DV)
        for ci in range(q_ref.shape[0] // c):
            rows = slice(ci * c, (ci + 1) * c)
            qc = q_ref[rows, qk_cols]
            kc = k_ref[rows, qk_cols]
            vc = v_ref[rows, v_cols]
            s = s_ref[h]
            att = _dot_nt(qc, kc) * intra
            o = _dot(att.astype(BF16), vc) + _dot((qc.astype(F32) * q_dec).astype(BF16), s.astype(BF16))
            kd_t = (kc.astype(F32) * k_dec).T.astype(BF16)
            s_ref[h] = s * chunk_dec + _dot(kd_t, vc)
            o = o * lax.rsqrt(jnp.mean(o * o, axis=-1, keepdims=True) + EPS)
            y_scr[rows, v_cols] = (_silu(g_ref[rows, v_cols].astype(F32)) * o).astype(BF16)
    xo_ref[...] = x_ref[...] + _dot(y_scr[...], w_ref[...])


def _ret_scan(qkvg, x, w_out, batch, seq, tm):
    t = x.shape[0]
    nj = seq // tm
    return pl.pallas_call(
        _ret_scan_kernel,
        grid=(batch, nj),
        in_specs=[
            pl.BlockSpec((tm, RET_QK), lambda b, j: (b * nj + j, 0)),
            pl.BlockSpec((tm, RET_QK), lambda b, j: (b * nj + j, 1)),
            pl.BlockSpec((tm, RET_V), lambda b, j: (b * nj + j, 1)),
            pl.BlockSpec((tm, RET_V), lambda b, j: (b * nj + j, 2)),
            pl.BlockSpec((tm, D_MODEL), lambda b, j: (b * nj + j, 0)),
            _resident((RET_V, D_MODEL)),
        ],
        out_specs=[
            pl.BlockSpec((tm, D_MODEL), lambda b, j: (b * nj + j, 0)),
            pl.BlockSpec((None, RET_HEADS, RET_DK, RET_DV), lambda b, j: (b, 0, 0, 0)),
        ],
        out_shape=[
            jax.ShapeDtypeStruct((t, D_MODEL), F32),
            jax.ShapeDtypeStruct((batch, RET_HEADS, RET_DK, RET_DV), F32),
        ],
        scratch_shapes=[
            pltpu.VMEM((tm, RET_V), BF16),
            pltpu.VMEM((RET_HEADS, SCAN_CHUNK, SCAN_CHUNK), F32),
        ],
        compiler_params=_params("arbitrary", "arbitrary"),
        name="ret_scan",
    )(qkvg, qkvg, qkvg, qkvg, x, w_out)


def _ret_rows(q_ref, k_ref, v_ref, g_ref, s_ref, so_ref, y_ref):
    n = q_ref.shape[0]
    pad = jnp.zeros((SUBLANES - n, RET_DK), F32)
    v = v_ref[...]
    gate = _silu(g_ref[...])
    for h in range(RET_HEADS):
        qk_cols = slice(h * RET_DK, (h + 1) * RET_DK)
        v_cols = slice(h * RET_DV, (h + 1) * RET_DV)
        q_t = jnp.concatenate([q_ref[:, qk_cols], pad], axis=0).T
        k_t = jnp.concatenate([k_ref[:, qk_cols], pad], axis=0).T
        for i in range(n):
            s_new = s_ref[i, h] * RET_GAMMA[h] + k_t[:, i:i + 1] * v[i:i + 1, v_cols]
            so_ref[i, h] = s_new
            o = jnp.sum(q_t[:, i:i + 1] * s_new, axis=0, keepdims=True)
            o = o * lax.rsqrt(jnp.mean(o * o, axis=-1, keepdims=True) + EPS)
            y_ref[i:i + 1, v_cols] = gate[i:i + 1, v_cols] * o


def _ret_rows_specs(rows, step):
    state = pl.BlockSpec((None, rows, RET_HEADS, RET_DK, RET_DV), lambda *ids: (0, step(*ids), 0, 0, 0))
    ins = [
        pl.BlockSpec((None, rows, RET_QK), lambda *ids: (step(*ids), 0, 0)),
        pl.BlockSpec((None, rows, RET_QK), lambda *ids: (step(*ids), 0, 1)),
        pl.BlockSpec((None, rows, RET_V), lambda *ids: (step(*ids), 0, 1)),
        pl.BlockSpec((None, rows, RET_V), lambda *ids: (step(*ids), 0, 2)),
        state,
    ]
    outs = [state, pl.BlockSpec((None, rows, RET_V), lambda *ids: (step(*ids), 0, 0))]
    return ins, outs


def _norm_matmul_kernel(x_ref, g_ref, w_ref, o_ref):
    o_ref[...] = _dot(_rms(x_ref[...], g_ref[...]).astype(BF16), w_ref[...])


def _norm_matmul(x, g, w):
    return pl.pallas_call(
        _norm_matmul_kernel,
        out_shape=jax.ShapeDtypeStruct((x.shape[0], w.shape[1]), F32),
        compiler_params=pltpu.CompilerParams(vmem_limit_bytes=VMEM_LIMIT),
        name="norm_matmul",
    )(x, g, w)


def _matmul_res_kernel(y_ref, w_ref, x_ref, o_ref):
    o_ref[...] = x_ref[...] + _dot(y_ref[...].astype(BF16), w_ref[...])


def _matmul_res(y, w, x):
    return pl.pallas_call(
        _matmul_res_kernel,
        out_shape=jax.ShapeDtypeStruct(x.shape, F32),
        compiler_params=pltpu.CompilerParams(vmem_limit_bytes=VMEM_LIMIT),
        name="matmul_res",
    )(y, w, x)


def _mem_kv_kernel(m_ref, g_ref, w_ref, k_ref, v_ref):
    hn = _rms(m_ref[...], g_ref[...]).astype(BF16)
    k_ref[...] = _dot(hn, w_ref[:, :D_MODEL])
    v_ref[...] = _dot(hn, w_ref[:, D_MODEL:])


def _mem_kv(mem, g, w, tm):
    t = mem.shape[0]
    out = jax.ShapeDtypeStruct((DEPTH, t, D_MODEL), F32)
    return pl.pallas_call(
        _mem_kv_kernel,
        grid=(DEPTH, t // tm),
        in_specs=[
            pl.BlockSpec((tm, D_MODEL), lambda i, m: (m, 0)),
            pl.BlockSpec((None, 1, D_MODEL), lambda i, m: (i, 0, 0)),
            pl.BlockSpec((None, D_MODEL, 2 * D_MODEL), lambda i, m: (i, 0, 0)),
        ],
        out_specs=[
            pl.BlockSpec((None, tm, D_MODEL), lambda i, m: (i, m, 0)),
            pl.BlockSpec((None, tm, D_MODEL), lambda i, m: (i, m, 0)),
        ],
        out_shape=[out, out],
        compiler_params=_params("arbitrary", "arbitrary"),
        name="mem_kv",
    )(mem, g, w)


def _xattn_rows_specs(rows, mem_len, layer, step):
    row = pl.BlockSpec((rows, XA_HEADS, XA_HD), lambda *ids: (step(*ids), 0, 0))
    mem = pl.BlockSpec((None, rows, mem_len, XA_HEADS, XA_HD), lambda *ids: (layer, step(*ids), 0, 0, 0))
    return [row, mem, mem], row


def _xattn_row(q_ref, mk_ref, mv_ref, o_ref, i):
    mem = mk_ref.shape[1]
    k2 = mk_ref[i].reshape(mem // 2, 2 * XA_HEADS, XA_HD)
    v2 = mv_ref[i].reshape(mem // 2, 2 * XA_HEADS, XA_HD)
    q = q_ref[i]
    s = jnp.sum(k2 * jnp.concatenate([q, q], axis=0)[None], axis=-1, keepdims=True) * XA_SCALE
    m = jnp.max(s, axis=0, keepdims=True)
    e = jnp.exp(s - jnp.maximum(m, pltpu.roll(m, XA_HEADS, 1)))
    l = jnp.sum(e, axis=0, keepdims=True)
    p = e / (l + pltpu.roll(l, XA_HEADS, 1))
    o2 = jnp.sum(p * v2, axis=0)
    o_ref[i] = o2[:XA_HEADS] + o2[XA_HEADS:]


def _xattn_main(x_ref, g_ref, wq_ref, mk_ref, mv_ref, wo_ref, xo_ref, o_scr, per_head=None):
    x = x_ref[...]
    q = _dot(_rms(x, g_ref[...]).astype(BF16), wq_ref[...])
    for h in range(XA_HEADS):
        cols = slice(h * XA_HD, (h + 1) * XA_HD)
        s = _dot_nt(q[:, cols].astype(BF16), mk_ref[:, cols].astype(BF16)) * XA_SCALE
        e = jnp.exp(s - jnp.max(s, axis=-1, keepdims=True))
        p = e / jnp.sum(e, axis=-1, keepdims=True)
        o_scr[:, cols] = _dot(p.astype(BF16), mv_ref[:, cols].astype(BF16)).astype(BF16)
        if per_head is not None:
            per_head(h)
    xo_ref[...] = x + _dot(o_scr[...], wo_ref[...])


def _xattn_kernel(x_ref, g_ref, wq_ref, mk_ref, mv_ref, wo_ref, xo_ref, o_scr):
    _xattn_main(x_ref, g_ref, wq_ref, mk_ref, mv_ref, wo_ref, xo_ref, o_scr)


def _xattn_mem_kernel(x_ref, g_ref, wq_ref, mk_ref, mv_ref, wo_ref, qs_ref, ck_ref, cv_ref,
                      xo_ref, os_ref, o_scr):
    rows = qs_ref.shape[0]
    assert rows <= XA_HEADS

    def per_head(h):
        if h < rows:
            _xattn_row(qs_ref, ck_ref, cv_ref, os_ref, h)

    _xattn_main(x_ref, g_ref, wq_ref, mk_ref, mv_ref, wo_ref, xo_ref, o_scr, per_head)


def _xattn(x, g, wq, mk, mv, wo, layer, batch, seq, tm, mem_step=None):
    t = x.shape[0]
    nj = seq // tm
    mem_len = mk.shape[2]
    step = lambda b, j: b * nj + j
    mem_spec = pl.BlockSpec((None, None, mem_len, D_MODEL), lambda b, j: (layer, b, 0, 0))
    row_spec = pl.BlockSpec((tm, D_MODEL), lambda b, j: (step(b, j), 0))
    in_specs = [row_spec, _resident((1, D_MODEL)), _resident((D_MODEL, D_MODEL)), mem_spec, mem_spec,
                _resident((D_MODEL, D_MODEL))]
    out_specs = [row_spec]
    out_shape = [jax.ShapeDtypeStruct((t, D_MODEL), F32)]
    args = [x, g, wq, mk, mv, wo]
    body = _xattn_kernel
    if mem_step is not None:
        q_s, cache_k, cache_v, cache_layer = mem_step
        rows = q_s.shape[0] // (batch * nj)
        assert rows * batch * nj == q_s.shape[0]
        side_in, side_out = _xattn_rows_specs(rows, cache_k.shape[2], cache_layer, step)
        in_specs += side_in
        out_specs += [side_out]
        out_shape += [jax.ShapeDtypeStruct(q_s.shape, F32)]
        args += [q_s, cache_k, cache_v]
        body = _xattn_mem_kernel
    return pl.pallas_call(
        body,
        grid=(batch, nj),
        in_specs=in_specs,
        out_specs=out_specs,
        out_shape=out_shape,
        scratch_shapes=[pltpu.VMEM((tm, D_MODEL), BF16)],
        compiler_params=_params("arbitrary", "arbitrary"),
        name="xattn",
    )(*args)


def _ffn_tail(x, y_scr, wdn_ref, gf_ref, xo_ref, final):
    out = x + _dot(y_scr[...], wdn_ref[...])
    if final:
        out = _rms(out, gf_ref[...])
    xo_ref[...] = out


def _ffn_mem_kernel(x_ref, g_ref, wup_ref, cw_ref, cb_ref, wdn_ref, gf_ref, qs_ref, ck_ref, cv_ref,
                    xo_ref, buf_ref, os_ref, carry_scr, y_scr, *, final):
    def per_chunk(n):
        if n < qs_ref.shape[0]:
            _xattn_row(qs_ref, ck_ref, cv_ref, os_ref, n)

    _ffn_kernel(x_ref, g_ref, wup_ref, cw_ref, cb_ref, wdn_ref, gf_ref, xo_ref, buf_ref, carry_scr, y_scr,
                final=final, per_chunk=per_chunk)


def _ffn_kernel(x_ref, g_ref, wup_ref, cw_ref, cb_ref, wdn_ref, gf_ref, xo_ref, buf_ref,
                carry_scr, y_scr, *, final, per_chunk=None):
    @pl.when(pl.program_id(1) == 0)
    def _():
        carry_scr[...] = jnp.zeros_like(carry_scr)

    x = x_ref[...]
    tm = x.shape[0]
    hn = _rms(x, g_ref[...]).astype(BF16)
    for n in range(FFN_DIM // FFN_COLS):
        cols = slice(n * FFN_COLS, (n + 1) * FFN_COLS)
        gcols = slice(FFN_DIM + n * FFN_COLS, FFN_DIM + (n + 1) * FFN_COLS)
        u = _dot(hn, wup_ref[:, cols])
        gate = _dot(hn, wup_ref[:, gcols])
        prev = carry_scr[:, cols]
        uc = cb_ref[:, cols] + _shift_rows(u, prev, 2) * cw_ref[0:1, cols]
        uc = uc + _shift_rows(u, prev, 1) * cw_ref[1:2, cols]
        uc = uc + u * cw_ref[2:3, cols]
        y_scr[:, cols] = (_gelu(uc) * gate).astype(BF16)
        carry_scr[:, cols] = u[tm - SUBLANES:, :]
        buf_ref[:, cols] = u[tm - (FFN_CONV - 1):, :]
        if per_chunk is not None:
            per_chunk(n)
    _ffn_tail(x, y_scr, wdn_ref, gf_ref, xo_ref, final)


def _ffn(x, g, wup, cw, cb, wdn, gf, batch, seq, tm, final, mem_step=None):
    t = x.shape[0]
    nj = seq // tm
    step = lambda b, j: b * nj + j
    side_in, side_out, side_shape, side_args, body = [], [], [], [], _ffn_kernel
    if mem_step is not None:
        q_s, cache_k, cache_v, cache_layer = mem_step
        rows = q_s.shape[0] // (batch * nj)
        assert rows * batch * nj == q_s.shape[0] and rows <= FFN_DIM // FFN_COLS
        side_in, out = _xattn_rows_specs(rows, cache_k.shape[2], cache_layer, step)
        side_out, side_shape, side_args = [out], [jax.ShapeDtypeStruct(q_s.shape, F32)], [q_s, cache_k, cache_v]
        body = _ffn_mem_kernel
    return pl.pallas_call(
        functools.partial(body, final=final),
        grid=(batch, nj),
        in_specs=[
            pl.BlockSpec((tm, D_MODEL), lambda b, j: (step(b, j), 0)),
            _resident((1, D_MODEL)),
            _resident((D_MODEL, 2 * FFN_DIM)),
            _resident((FFN_CONV, FFN_DIM)),
            _resident((1, FFN_DIM)),
            _resident((FFN_DIM, D_MODEL)),
            _resident((1, D_MODEL)),
        ] + side_in,
        out_specs=[
            pl.BlockSpec((tm, D_MODEL), lambda b, j: (step(b, j), 0)),
            pl.BlockSpec((None, FFN_CONV - 1, FFN_DIM), lambda b, j: (b, 0, 0)),
        ] + side_out,
        out_shape=[
            jax.ShapeDtypeStruct((t, D_MODEL), F32),
            jax.ShapeDtypeStruct((batch, FFN_CONV - 1, FFN_DIM), F32),
        ] + side_shape,
        scratch_shapes=[pltpu.VMEM((SUBLANES, FFN_DIM), F32), pltpu.VMEM((tm, FFN_DIM), BF16)],
        compiler_params=_params("arbitrary", "arbitrary"),
        name="ffn",
    )(x, g, wup, cw, cb, wdn, gf, *side_args)


def _ffn_step_kernel(x_ref, g_ref, wup_ref, cw_ref, cb_ref, wdn_ref, gf_ref, buf_ref, xo_ref, nbuf_ref,
                     y_scr, *, final):
    x = x_ref[...]
    hn = _rms(x, g_ref[...]).astype(BF16)
    for n in range(FFN_DIM // FFN_COLS):
        cols = slice(n * FFN_COLS, (n + 1) * FFN_COLS)
        gcols = slice(FFN_DIM + n * FFN_COLS, FFN_DIM + (n + 1) * FFN_COLS)
        u = _dot(hn, wup_ref[:, cols])
        gate = _dot(hn, wup_ref[:, gcols])
        b0 = buf_ref[:, cols]
        b1 = buf_ref[:, gcols]
        uc = cb_ref[:, cols] + b0 * cw_ref[0:1, cols]
        uc = uc + b1 * cw_ref[1:2, cols]
        uc = uc + u * cw_ref[2:3, cols]
        y_scr[:, cols] = (_gelu(uc) * gate).astype(BF16)
        nbuf_ref[:, cols] = b1
        nbuf_ref[:, gcols] = u
    _ffn_tail(x, y_scr, wdn_ref, gf_ref, xo_ref, final)


def _ffn_step(x, g, wup, cw, cb, wdn, gf, buf, final):
    t = x.shape[0]
    return pl.pallas_call(
        functools.partial(_ffn_step_kernel, final=final),
        out_shape=[
            jax.ShapeDtypeStruct((t, D_MODEL), F32),
            jax.ShapeDtypeStruct(buf.shape, F32),
        ],
        scratch_shapes=[pltpu.VMEM((t, FFN_DIM), BF16)],
        compiler_params=pltpu.CompilerParams(vmem_limit_bytes=VMEM_LIMIT),
        name="ffn_step",
    )(x, g, wup, cw, cb, wdn, gf, buf)


def _lru_gates(xc, wa_ref, ba_ref, wx_ref, bx_ref, lam_ref, a_out, u_out):
    xcb = xc.astype(BF16)
    log_s = _log_sigmoid(lam_ref[...])
    for blk in range(LRU_BLOCKS):
        cols = slice(blk * LRU_BW, (blk + 1) * LRU_BW)
        r = jax.nn.sigmoid(_dot(xcb[:, cols], wa_ref[blk]) + ba_ref[:, cols])
        i = jax.nn.sigmoid(_dot(xcb[:, cols], wx_ref[blk]) + bx_ref[:, cols])
        log_a = LRU_C * r * log_s[:, cols]
        a = jnp.exp(log_a)
        a_out[:, cols] = a
        u_out[:, cols] = jnp.sqrt(-jnp.tanh(log_a) * (a * a + 1.0)) * (i * xc[:, cols])


def _lru_kernel(x_ref, g_ref, win_ref, cw_ref, cb_ref, wa_ref, ba_ref, wx_ref, bx_ref, lam_ref, wout_ref,
                q_ref, k_ref, v_ref, gt_ref, s_ref, xo_ref, h_ref, buf_ref, so_ref, y_ref,
                carry_scr, h_scr, a_scr, u_scr):
    @pl.when(pl.program_id(1) == 0)
    def _():
        carry_scr[...] = jnp.zeros_like(carry_scr)
        h_scr[...] = jnp.zeros_like(h_scr)

    x = x_ref[...]
    tm = x.shape[0]
    hn = _rms(x, g_ref[...]).astype(BF16)
    xb = _dot(hn, win_ref[:, :LRU_WIDTH])
    prev = carry_scr[...]
    xc = cb_ref[...] + _shift_rows(xb, prev, 3) * cw_ref[0:1, :]
    xc = xc + _shift_rows(xb, prev, 2) * cw_ref[1:2, :]
    xc = xc + _shift_rows(xb, prev, 1) * cw_ref[2:3, :]
    xc = xc + xb * cw_ref[3:4, :]
    carry_scr[...] = xb[tm - SUBLANES:, :]
    buf_ref[...] = xb[tm - (LRU_CONV - 1):, :]
    _lru_gates(xc, wa_ref, ba_ref, wx_ref, bx_ref, lam_ref, a_scr, u_scr)

    sub = lax.broadcasted_iota(jnp.int32, (SUBLANES, LRU_WIDTH), 0)

    h = h_scr[...]
    for gi in range(tm // SUBLANES):
        rows = slice(gi * SUBLANES, (gi + 1) * SUBLANES)
        a = a_scr[rows, :]
        u = u_scr[rows, :]
        for s in (1, 2, 4):
            keep = sub >= s
            u = jnp.where(keep, a * pltpu.roll(u, s, 0) + u, u)
            a = jnp.where(keep, a * pltpu.roll(a, s, 0), a)
        hs = a * h + u
        u_scr[rows, :] = hs
        h = hs[SUBLANES - 1:, :]
    h_scr[...] = h
    h_ref[...] = h
    gb = _dot(hn, win_ref[:, LRU_WIDTH:])
    y = (_gelu(gb) * u_scr[...]).astype(BF16)
    xo_ref[...] = x + _dot(y, wout_ref[...])
    _ret_rows(q_ref, k_ref, v_ref, gt_ref, s_ref, so_ref, y_ref)


def _lru_weight_specs():
    return [
        _resident((1, D_MODEL)),
        _resident((D_MODEL, 2 * LRU_WIDTH)),
        _resident((LRU_CONV, LRU_WIDTH)),
        _resident((1, LRU_WIDTH)),
        _resident((LRU_BLOCKS, LRU_BW, LRU_BW)),
        _resident((1, LRU_WIDTH)),
        _resident((LRU_BLOCKS, LRU_BW, LRU_BW)),
        _resident((1, LRU_WIDTH)),
        _resident((1, LRU_WIDTH)),
        _resident((LRU_WIDTH, D_MODEL)),
    ]


def _lru(x, weights, qkvg_s, state, batch, seq, tm):
    t = x.shape[0]
    nj = seq // tm
    steps, rows, _ = qkvg_s.shape
    assert steps == batch * nj and steps * rows == state.shape[1]
    ret_in, ret_out = _ret_rows_specs(rows, lambda b, j: b * nj + j)
    return pl.pallas_call(
        _lru_kernel,
        grid=(batch, nj),
        in_specs=[pl.BlockSpec((tm, D_MODEL), lambda b, j: (b * nj + j, 0))] + _lru_weight_specs() + ret_in,
        out_specs=[
            pl.BlockSpec((tm, D_MODEL), lambda b, j: (b * nj + j, 0)),
            pl.BlockSpec((None, 1, LRU_WIDTH), lambda b, j: (b, 0, 0)),
            pl.BlockSpec((None, LRU_CONV - 1, LRU_WIDTH), lambda b, j: (b, 0, 0)),
        ] + ret_out,
        out_shape=[
            jax.ShapeDtypeStruct((t, D_MODEL), F32),
            jax.ShapeDtypeStruct((batch, 1, LRU_WIDTH), F32),
            jax.ShapeDtypeStruct((batch, LRU_CONV - 1, LRU_WIDTH), F32),
            jax.ShapeDtypeStruct(state.shape, F32),
            jax.ShapeDtypeStruct((steps, rows, RET_V), F32),
        ],
        scratch_shapes=[
            pltpu.VMEM((SUBLANES, LRU_WIDTH), F32),
            pltpu.VMEM((1, LRU_WIDTH), F32),
            pltpu.VMEM((tm, LRU_WIDTH), F32),
            pltpu.VMEM((tm, LRU_WIDTH), F32),
        ],
        compiler_params=_params("arbitrary", "arbitrary"),
        name="lru",
    )(x, *weights, qkvg_s, qkvg_s, qkvg_s, qkvg_s, state)


def _lru_step_kernel(x_ref, g_ref, win_ref, cw_ref, cb_ref, wa_ref, ba_ref, wx_ref, bx_ref, lam_ref, wout_ref,
                     h0_ref, buf_ref, xo_ref, h_ref, nbuf_ref, a_scr, u_scr):
    x = x_ref[...]
    w = LRU_WIDTH
    hn = _rms(x, g_ref[...]).astype(BF16)
    xb = _dot(hn, win_ref[:, :w])
    xc = cb_ref[...] + buf_ref[:, 0:w] * cw_ref[0:1, :]
    xc = xc + buf_ref[:, w:2 * w] * cw_ref[1:2, :]
    xc = xc + buf_ref[:, 2 * w:3 * w] * cw_ref[2:3, :]
    xc = xc + xb * cw_ref[3:4, :]
    nbuf_ref[:, 0:2 * w] = buf_ref[:, w:3 * w]
    nbuf_ref[:, 2 * w:3 * w] = xb
    _lru_gates(xc, wa_ref, ba_ref, wx_ref, bx_ref, lam_ref, a_scr, u_scr)
    hs = a_scr[...] * h0_ref[...] + u_scr[...]
    h_ref[...] = hs
    gb = _dot(hn, win_ref[:, w:])
    xo_ref[...] = x + _dot((_gelu(gb) * hs).astype(BF16), wout_ref[...])


def _lru_step(x, weights, h0, buf):
    t = x.shape[0]
    return pl.pallas_call(
        _lru_step_kernel,
        out_shape=[
            jax.ShapeDtypeStruct((t, D_MODEL), F32),
            jax.ShapeDtypeStruct(h0.shape, F32),
            jax.ShapeDtypeStruct(buf.shape, F32),
        ],
        scratch_shapes=[pltpu.VMEM((t, LRU_WIDTH), F32), pltpu.VMEM((t, LRU_WIDTH), F32)],
        compiler_params=pltpu.CompilerParams(vmem_limit_bytes=VMEM_LIMIT),
        name="lru_step",
    )(x, *weights, h0, buf)


def _rope_tables(positions):
    inv = ROPE_BASE ** (-jnp.arange(ROPE_HALF, dtype=F32) / ROPE_HALF)
    ang = positions[:, None] * inv[None, :]
    return jnp.cos(ang), jnp.sin(ang)


def kernel(x_prompt, x_sample, state_ret, state_lru_h, state_lru_conv, state_ffn_conv, cache_mem_k, cache_mem_v, mem_prompt, norm_mix, norm_xa, norm_mem, norm_ffn, norm_final, ret_w_in, ret_w_out, lru_w_in, lru_conv_w, lru_conv_b, lru_wa, lru_ba, lru_wx, lru_bx, lru_lambda, lru_w_out, xa_w_q, xa_w_kv, xa_w_o, ffn_w_up, ffn_conv_w, ffn_conv_b, ffn_w_down):
    bp, seq, d = x_prompt.shape
    bs = x_sample.shape[0]
    mem_len = mem_prompt.shape[1]
    assert d == D_MODEL and x_sample.shape[1] == 1
    assert seq % WIDE_ROW_TILE == 0 and WIDE_ROW_TILE % ROW_TILE == 0
    assert ROW_TILE % SCAN_CHUNK == 0

    row = lambda v: v.reshape(1, -1)
    ret_w_in_b = ret_w_in[0].astype(BF16)
    ret_w_out_b = ret_w_out[0].astype(BF16)
    xa_w_q_b = xa_w_q.astype(BF16)
    xa_w_kv_b = xa_w_kv.astype(BF16)
    xa_w_o_b = xa_w_o.astype(BF16)
    ffn_w_up_b = ffn_w_up.astype(BF16)
    ffn_w_down_b = ffn_w_down.astype(BF16)
    lru_weights = lambda j: (
        row(norm_mix[1]), lru_w_in[j].astype(BF16), lru_conv_w[j], row(lru_conv_b[j]),
        lru_wa[j].astype(BF16), row(lru_ba[j]), lru_wx[j].astype(BF16), row(lru_bx[j]),
        row(lru_lambda[j]), lru_w_out[j].astype(BF16))
    gf = row(norm_final)

    steps = bp * (seq // ROW_TILE)
    heads = lambda v: v.reshape(bs, XA_HEADS, XA_HD)
    ffn_args = lambda i: (row(norm_ffn[i]), ffn_w_up_b[i], ffn_conv_w[i], row(ffn_conv_b[i]), ffn_w_down_b[i], gf)

    mem_k, mem_v = _mem_kv(mem_prompt.reshape(bp * mem_len, d), norm_mem.reshape(DEPTH, 1, d), xa_w_kv_b, ROW_TILE)
    mem_k = mem_k.reshape(DEPTH, bp, mem_len, d)
    mem_v = mem_v.reshape(DEPTH, bp, mem_len, d)
    cos_p, sin_p = _rope_tables(jnp.arange(seq, dtype=F32))
    cos_s, sin_s = _rope_tables(jnp.full((bs,), PAST_LEN, F32))
    xp = x_prompt.reshape(bp * seq, d)
    xs = x_sample.reshape(bs, d)

    qkvg = _ret_proj(xp, row(norm_mix[0]), ret_w_in_b, cos_p, sin_p, WIDE_ROW_TILE, BF16)
    qkvg_s = _ret_proj(xs, row(norm_mix[0]), ret_w_in_b, cos_s, sin_s, bs, F32).reshape(steps, bs // steps, RET_COLS)
    xp, ret_p = _ret_scan(qkvg, xp, ret_w_out_b, bp, seq, ROW_TILE)
    xp, = _xattn(xp, row(norm_xa[0]), xa_w_q_b[0], mem_k, mem_v, xa_w_o_b[0], 0, bp, seq, WIDE_ROW_TILE)
    xp, ffn_p0 = _ffn(xp, *ffn_args(0), bp, seq, WIDE_ROW_TILE, False)

    xp, lru_h_p, lru_conv_p, ret_s, y_s = _lru(xp, lru_weights(0), qkvg_s, state_ret, bp, seq, ROW_TILE)
    xs = _matmul_res(y_s.reshape(bs, RET_V), ret_w_out_b, xs)
    q_s = heads(_norm_matmul(xs, row(norm_xa[0]), xa_w_q_b[0]))
    xp, o_s = _xattn(xp, row(norm_xa[1]), xa_w_q_b[1], mem_k, mem_v, xa_w_o_b[1], 1, bp, seq, ROW_TILE,
                     mem_step=(q_s, cache_mem_k, cache_mem_v, 0))
    xs = _matmul_res(o_s.reshape(bs, d), xa_w_o_b[0], xs)
    xs, ffn_s0 = _ffn_step(xs, *ffn_args(0), state_ffn_conv[0].reshape(bs, -1), False)
    xs, lru_h_s, lru_conv_s = _lru_step(xs, lru_weights(0), state_lru_h[0], state_lru_conv[0].reshape(bs, -1))
    q_s = heads(_norm_matmul(xs, row(norm_xa[1]), xa_w_q_b[1]))
    xp, ffn_p1, o_s = _ffn(xp, *ffn_args(1), bp, seq, ROW_TILE, True, mem_step=(q_s, cache_mem_k, cache_mem_v, 1))
    xs = _matmul_res(o_s.reshape(bs, d), xa_w_o_b[1], xs)
    xs, ffn_s1 = _ffn_step(xs, *ffn_args(1), state_ffn_conv[1].reshape(bs, -1), True)

    mem_shape = (DEPTH, bp, mem_len, XA_HEADS, XA_HD)
    return (
        xp.reshape(bp, seq, d),
        xs.reshape(bs, 1, d),
        ret_p[None],
        ret_s,
        lru_h_p.reshape(1, bp, LRU_WIDTH),
        lru_h_s[None],
        lru_conv_p[None],
        lru_conv_s.reshape(1, bs, LRU_CONV - 1, LRU_WIDTH),
        jnp.stack([ffn_p0, ffn_p1]),
        jnp.stack([ffn_s0, ffn_s1]).reshape(DEPTH, bs, FFN_CONV - 1, FFN_DIM),
        mem_k.reshape(mem_shape),
        mem_v.reshape(mem_shape),
    )
```

```python
import functools
import math

import jax
import jax.numpy as jnp
from jax import lax
from jax.experimental import pallas as pl
from jax.experimental.pallas import tpu as pltpu

F32 = jnp.float32
BF16 = jnp.bfloat16

D_MODEL = 1024
DEPTH = 2
PAST_LEN = 16384
RET_HEADS = 4
RET_DK = D_MODEL // RET_HEADS
RET_DV = 2 * D_MODEL // RET_HEADS
ROPE_BASE = 10000.0
ROPE_HALF = RET_DK // 2
LRU_WIDTH = D_MODEL
LRU_BLOCKS = 4
LRU_BW = LRU_WIDTH // LRU_BLOCKS
LRU_CONV = 4
LRU_C = 8.0
XA_HEADS = 4
XA_HD = D_MODEL // XA_HEADS
FFN_DIM = 3 * D_MODEL
FFN_CONV = 3
EPS = 1e-6

RET_QK = RET_HEADS * RET_DK
RET_V = RET_HEADS * RET_DV
RET_COLS = 2 * RET_QK + 2 * RET_V
RET_GAMMA = tuple(1.0 - 2.0 ** (-5.0 - h) for h in range(RET_HEADS))
RET_LOG_G = tuple(math.log(g) for g in RET_GAMMA)
K_SCALE = RET_DK ** -0.5
XA_SCALE = XA_HD ** -0.5
SQRT_2_OVER_PI = math.sqrt(2.0 / math.pi)

SUBLANES = 8
BF16_SUBLANES = 16
VMEM_LIMIT = 56 << 20

ROW_TILE = 512
WIDE_ROW_TILE = 1024
SCAN_CHUNK = 256
FFN_COLS = 512


def _params(*sem):
    return pltpu.CompilerParams(dimension_semantics=sem, vmem_limit_bytes=VMEM_LIMIT)


def _resident(shape):
    zeros = (0,) * len(shape)
    return pl.BlockSpec(shape, lambda *_: zeros, pipeline_mode=pl.Buffered(1))


def _dot(a, b):
    return jnp.dot(a, b, preferred_element_type=F32)


def _dot_nt(a, b):
    return lax.dot_general(a, b, (((1,), (1,)), ((), ())), preferred_element_type=F32)


def _rms(x, g):
    return x * lax.rsqrt(jnp.mean(x * x, axis=-1, keepdims=True) + EPS) * g


def _gelu(x):
    return x * (0.5 * (1.0 + jnp.tanh(SQRT_2_OVER_PI * (x + 0.044715 * (x * x * x)))))


def _silu(x):
    return x * jax.nn.sigmoid(x)


def _log_sigmoid(x):
    return jnp.minimum(x, 0.0) - jnp.log(1.0 + jnp.exp(-jnp.abs(x)))


def _shift_rows(x, prev, s):
    rolled = pltpu.roll(x, s, 0)
    rows = lax.broadcasted_iota(jnp.int32, (SUBLANES, 1), 0)
    head = jnp.where(rows < s, pltpu.roll(prev, s, 0), rolled[:SUBLANES])
    return jnp.concatenate([head, rolled[SUBLANES:]], axis=0)


def _cast_specs(src, layer, steps, step):
    _, r, n = src.shape
    rows = r // steps
    assert rows * steps == r and rows % BF16_SUBLANES == 0
    return (pl.BlockSpec((None, rows, n), lambda *ids: (layer, step(*ids), 0)),
            pl.BlockSpec((rows, n), lambda *ids: (step(*ids), 0)),
            jax.ShapeDtypeStruct((r, n), BF16))


def _ret_proj_kernel(x_ref, g_ref, w_ref, cos_ref, sin_ref, o_ref):
    hn = _rms(x_ref[...], g_ref[...]).astype(BF16)
    cos = cos_ref[...]
    sin = sin_ref[...]
    for n in range(2 * RET_HEADS):
        c0 = n * RET_DK
        acc = _dot(hn, w_ref[:, c0:c0 + RET_DK])
        x1 = acc[:, :ROPE_HALF]
        x2 = acc[:, ROPE_HALF:]
        r1 = x1 * cos - x2 * sin
        r2 = x1 * sin + x2 * cos
        if n >= RET_HEADS:
            r1 = r1 * K_SCALE
            r2 = r2 * K_SCALE
        o_ref[:, c0:c0 + ROPE_HALF] = r1.astype(o_ref.dtype)
        o_ref[:, c0 + ROPE_HALF:c0 + RET_DK] = r2.astype(o_ref.dtype)
    for n in range(2 * RET_V // RET_DV):
        c0 = 2 * RET_QK + n * RET_DV
        o_ref[:, c0:c0 + RET_DV] = _dot(hn, w_ref[:, c0:c0 + RET_DV]).astype(o_ref.dtype)


def _ret_proj(x, g, w, cos, sin, tm, out_dtype):
    t = x.shape[0]
    period = cos.shape[0] // tm
    return pl.pallas_call(
        _ret_proj_kernel,
        grid=(t // tm,),
        in_specs=[
            pl.BlockSpec((tm, D_MODEL), lambda m: (m, 0)),
            _resident((1, D_MODEL)),
            _resident((D_MODEL, RET_COLS)),
            pl.BlockSpec((tm, ROPE_HALF), lambda m: (m % period, 0)),
            pl.BlockSpec((tm, ROPE_HALF), lambda m: (m % period, 0)),
        ],
        out_specs=pl.BlockSpec((tm, RET_COLS), lambda m: (m, 0)),
        out_shape=jax.ShapeDtypeStruct((t, RET_COLS), out_dtype),
        compiler_params=_params("arbitrary"),
        name="ret_proj",
    )(x, g, w, cos, sin)


def _ret_scan_kernel(q_ref, k_ref, v_ref, g_ref, x_ref, w_ref, wup_ref, wdn_ref,
                     xo_ref, s_ref, wup_b_ref, wdn_b_ref, y_scr, intra_scr):
    c = SCAN_CHUNK

    @pl.when((pl.program_id(0) == 0) & (pl.program_id(1) == 0))
    def _():
        row = lax.broadcasted_iota(jnp.int32, (c, c), 0).astype(F32)
        col = lax.broadcasted_iota(jnp.int32, (c, c), 1).astype(F32)
        rel = row - col
        for h in range(RET_HEADS):
            intra_scr[h] = jnp.where(rel >= 0, jnp.exp(RET_LOG_G[h] * jnp.maximum(rel, 0.0)), 0.0)

    @pl.when(pl.program_id(1) == 0)
    def _():
        s_ref[...] = jnp.zeros_like(s_ref)

    idx = lax.broadcasted_iota(jnp.int32, (c, 1), 0).astype(F32)
    for h in range(RET_HEADS):
        lg = RET_LOG_G[h]
        intra = intra_scr[h]
        q_dec = jnp.exp(lg * (idx + 1.0))
        k_dec = jnp.exp(lg * (c - 1.0 - idx))
        chunk_dec = math.exp(lg * c)
        qk_cols = slice(h * RET_DK, (h + 1) * RET_DK)
        v_cols = slice(h * RET_DV, (h + 1) * RET_DV)
        for ci in range(q_ref.shape[0] // c):
            rows = slice(ci * c, (ci + 1) * c)
            qc = q_ref[rows, qk_cols]
            kc = k_ref[rows, qk_cols]
            vc = v_ref[rows, v_cols]
            s = s_ref[h]
            att = _dot_nt(qc, kc) * intra
            o = _dot(att.astype(BF16), vc) + _dot((qc.astype(F32) * q_dec).astype(BF16), s.astype(BF16))
            kd_t = (kc.astype(F32) * k_dec).T.astype(BF16)
            s_ref[h] = s * chunk_dec + _dot(kd_t, vc)
            o = o * lax.rsqrt(jnp.mean(o * o, axis=-1, keepdims=True) + EPS)
            y_scr[rows, v_cols] = (_silu(g_ref[rows, v_cols].astype(F32)) * o).astype(BF16)
    xo_ref[...] = x_ref[...] + _dot(y_scr[...], w_ref[...])
    wup_b_ref[...] = wup_ref[...].astype(BF16)
    wdn_b_ref[...] = wdn_ref[...].astype(BF16)


def _ret_scan(qkvg, x, w_out, ffn_w_up, ffn_w_down, batch, seq, tm):
    t = x.shape[0]
    nj = seq // tm
    step = lambda b, j: b * nj + j
    wup_in, wup_out, wup_shape = _cast_specs(ffn_w_up, 0, batch * nj, step)
    wdn_in, wdn_out, wdn_shape = _cast_specs(ffn_w_down, 0, batch * nj, step)
    return pl.pallas_call(
        _ret_scan_kernel,
        grid=(batch, nj),
        in_specs=[
            pl.BlockSpec((tm, RET_QK), lambda b, j: (b * nj + j, 0)),
            pl.BlockSpec((tm, RET_QK), lambda b, j: (b * nj + j, 1)),
            pl.BlockSpec((tm, RET_V), lambda b, j: (b * nj + j, 1)),
            pl.BlockSpec((tm, RET_V), lambda b, j: (b * nj + j, 2)),
            pl.BlockSpec((tm, D_MODEL), lambda b, j: (b * nj + j, 0)),
            _resident((RET_V, D_MODEL)),
            wup_in,
            wdn_in,
        ],
        out_specs=[
            pl.BlockSpec((tm, D_MODEL), lambda b, j: (b * nj + j, 0)),
            pl.BlockSpec((None, RET_HEADS, RET_DK, RET_DV), lambda b, j: (b, 0, 0, 0)),
            wup_out,
            wdn_out,
        ],
        out_shape=[
            jax.ShapeDtypeStruct((t, D_MODEL), F32),
            jax.ShapeDtypeStruct((batch, RET_HEADS, RET_DK, RET_DV), F32),
            wup_shape,
            wdn_shape,
        ],
        scratch_shapes=[
            pltpu.VMEM((tm, RET_V), BF16),
            pltpu.VMEM((RET_HEADS, SCAN_CHUNK, SCAN_CHUNK), F32),
        ],
        compiler_params=_params("arbitrary", "arbitrary"),
        name="ret_scan",
    )(qkvg, qkvg, qkvg, qkvg, x, w_out, ffn_w_up, ffn_w_down)


def _ret_rows(q_ref, k_ref, v_ref, g_ref, s_ref, so_ref, y_ref):
    n = q_ref.shape[0]
    pad = jnp.zeros((SUBLANES - n, RET_DK), F32)
    v = v_ref[...]
    gate = _silu(g_ref[...])
    for h in range(RET_HEADS):
        qk_cols = slice(h * RET_DK, (h + 1) * RET_DK)
        v_cols = slice(h * RET_DV, (h + 1) * RET_DV)
        q_t = jnp.concatenate([q_ref[:, qk_cols], pad], axis=0).T
        k_t = jnp.concatenate([k_ref[:, qk_cols], pad], axis=0).T
        for i in range(n):
            s_new = s_ref[i, h] * RET_GAMMA[h] + k_t[:, i:i + 1] * v[i:i + 1, v_cols]
            so_ref[i, h] = s_new
            o = jnp.sum(q_t[:, i:i + 1] * s_new, axis=0, keepdims=True)
            o = o * lax.rsqrt(jnp.mean(o * o, axis=-1, keepdims=True) + EPS)
            y_ref[i:i + 1, v_cols] = gate[i:i + 1, v_cols] * o


def _ret_rows_specs(rows, step):
    state = pl.BlockSpec((None, rows, RET_HEADS, RET_DK, RET_DV), lambda *ids: (0, step(*ids), 0, 0, 0))
    ins = [
        pl.BlockSpec((None, rows, RET_QK), lambda *ids: (step(*ids), 0, 0)),
        pl.BlockSpec((None, rows, RET_QK), lambda *ids: (step(*ids), 0, 1)),
        pl.BlockSpec((None, rows, RET_V), lambda *ids: (step(*ids), 0, 1)),
        pl.BlockSpec((None, rows, RET_V), lambda *ids: (step(*ids), 0, 2)),
        state,
    ]
    outs = [state, pl.BlockSpec((None, rows, RET_V), lambda *ids: (step(*ids), 0, 0))]
    return ins, outs


def _norm_matmul_kernel(x_ref, g_ref, w_ref, o_ref):
    o_ref[...] = _dot(_rms(x_ref[...], g_ref[...]).astype(BF16), w_ref[...])


def _norm_matmul(x, g, w):
    return pl.pallas_call(
        _norm_matmul_kernel,
        out_shape=jax.ShapeDtypeStruct((x.shape[0], w.shape[1]), F32),
        compiler_params=pltpu.CompilerParams(vmem_limit_bytes=VMEM_LIMIT),
        name="norm_matmul",
    )(x, g, w)


def _matmul_res_kernel(y_ref, w_ref, x_ref, o_ref):
    o_ref[...] = x_ref[...] + _dot(y_ref[...].astype(BF16), w_ref[...])


def _matmul_res(y, w, x):
    return pl.pallas_call(
        _matmul_res_kernel,
        out_shape=jax.ShapeDtypeStruct(x.shape, F32),
        compiler_params=pltpu.CompilerParams(vmem_limit_bytes=VMEM_LIMIT),
        name="matmul_res",
    )(y, w, x)


def _mem_kv_kernel(m_ref, g_ref, w_ref, k_ref, v_ref, kh_ref, vh_ref):
    hn = _rms(m_ref[...], g_ref[...]).astype(BF16)
    k = _dot(hn, w_ref[:, :D_MODEL])
    v = _dot(hn, w_ref[:, D_MODEL:])
    k_ref[...] = k
    v_ref[...] = v
    kh_ref[...] = k.reshape(kh_ref.shape)
    vh_ref[...] = v.reshape(vh_ref.shape)


def _mem_kv(mem, g, w, tm):
    t = mem.shape[0]
    out = jax.ShapeDtypeStruct((DEPTH, t, D_MODEL), F32)
    out_heads = jax.ShapeDtypeStruct((DEPTH, t, XA_HEADS, XA_HD), F32)
    heads_spec = pl.BlockSpec((None, tm, XA_HEADS, XA_HD), lambda i, m: (i, m, 0, 0))
    return pl.pallas_call(
        _mem_kv_kernel,
        grid=(DEPTH, t // tm),
        in_specs=[
            pl.BlockSpec((tm, D_MODEL), lambda i, m: (m, 0)),
            pl.BlockSpec((None, 1, D_MODEL), lambda i, m: (i, 0, 0)),
            pl.BlockSpec((None, D_MODEL, 2 * D_MODEL), lambda i, m: (i, 0, 0)),
        ],
        out_specs=[
            pl.BlockSpec((None, tm, D_MODEL), lambda i, m: (i, m, 0)),
            pl.BlockSpec((None, tm, D_MODEL), lambda i, m: (i, m, 0)),
            heads_spec,
            heads_spec,
        ],
        out_shape=[out, out, out_heads, out_heads],
        compiler_params=_params("arbitrary", "arbitrary"),
        name="mem_kv",
    )(mem, g, w)


def _xattn_rows_specs(rows, mem_len, layer, step):
    row = pl.BlockSpec((rows, XA_HEADS, XA_HD), lambda *ids: (step(*ids), 0, 0))
    mem = pl.BlockSpec((None, rows, mem_len, XA_HEADS, XA_HD), lambda *ids: (layer, step(*ids), 0, 0, 0))
    return [row, mem, mem], row


def _xattn_row(q_ref, mk_ref, mv_ref, o_ref, i):
    mem = mk_ref.shape[1]
    k2 = mk_ref[i].reshape(mem // 2, 2 * XA_HEADS, XA_HD)
    v2 = mv_ref[i].reshape(mem // 2, 2 * XA_HEADS, XA_HD)
    q = q_ref[i]
    s = jnp.sum(k2 * jnp.concatenate([q, q], axis=0)[None], axis=-1, keepdims=True) * XA_SCALE
    m = jnp.max(s, axis=0, keepdims=True)
    e = jnp.exp(s - jnp.maximum(m, pltpu.roll(m, XA_HEADS, 1)))
    l = jnp.sum(e, axis=0, keepdims=True)
    p = e / (l + pltpu.roll(l, XA_HEADS, 1))
    o2 = jnp.sum(p * v2, axis=0)
    o_ref[i] = o2[:XA_HEADS] + o2[XA_HEADS:]


def _xattn_main(x_ref, g_ref, wq_ref, mk_ref, mv_ref, wo_ref, xo_ref, o_scr, per_head=None):
    x = x_ref[...]
    q = _dot(_rms(x, g_ref[...]).astype(BF16), wq_ref[...])
    for h in range(XA_HEADS):
        cols = slice(h * XA_HD, (h + 1) * XA_HD)
        s = _dot_nt(q[:, cols].astype(BF16), mk_ref[:, cols].astype(BF16)) * XA_SCALE
        e = jnp.exp(s - jnp.max(s, axis=-1, keepdims=True))
        p = e / jnp.sum(e, axis=-1, keepdims=True)
        o_scr[:, cols] = _dot(p.astype(BF16), mv_ref[:, cols].astype(BF16)).astype(BF16)
        if per_head is not None:
            per_head(h)
    xo_ref[...] = x + _dot(o_scr[...], wo_ref[...])


def _xattn_kernel(x_ref, g_ref, wq_ref, mk_ref, mv_ref, wo_ref, xo_ref, o_scr):
    _xattn_main(x_ref, g_ref, wq_ref, mk_ref, mv_ref, wo_ref, xo_ref, o_scr)


def _xattn_cast_kernel(x_ref, g_ref, wq_ref, mk_ref, mv_ref, wo_ref, wup_ref, wdn_ref,
                       xo_ref, wup_b_ref, wdn_b_ref, o_scr):
    _xattn_main(x_ref, g_ref, wq_ref, mk_ref, mv_ref, wo_ref, xo_ref, o_scr)
    wup_b_ref[...] = wup_ref[...].astype(BF16)
    wdn_b_ref[...] = wdn_ref[...].astype(BF16)


def _xattn_mem_kernel(x_ref, g_ref, wq_ref, mk_ref, mv_ref, wo_ref, qs_ref, ck_ref, cv_ref,
                      xo_ref, os_ref, o_scr):
    rows = qs_ref.shape[0]
    assert rows <= XA_HEADS

    def per_head(h):
        if h < rows:
            _xattn_row(qs_ref, ck_ref, cv_ref, os_ref, h)

    _xattn_main(x_ref, g_ref, wq_ref, mk_ref, mv_ref, wo_ref, xo_ref, o_scr, per_head)


def _xattn(x, g, wq, mk, mv, wo, layer, batch, seq, tm, mem_step=None, cast_ffn=None):
    t = x.shape[0]
    nj = seq // tm
    mem_len = mk.shape[2]
    step = lambda b, j: b * nj + j
    mem_spec = pl.BlockSpec((None, None, mem_len, D_MODEL), lambda b, j: (layer, b, 0, 0))
    row_spec = pl.BlockSpec((tm, D_MODEL), lambda b, j: (step(b, j), 0))
    in_specs = [row_spec, _resident((1, D_MODEL)), _resident((D_MODEL, D_MODEL)), mem_spec, mem_spec,
                _resident((D_MODEL, D_MODEL))]
    out_specs = [row_spec]
    out_shape = [jax.ShapeDtypeStruct((t, D_MODEL), F32)]
    args = [x, g, wq, mk, mv, wo]
    body = _xattn_kernel
    if mem_step is not None:
        q_s, cache_k, cache_v, cache_layer = mem_step
        rows = q_s.shape[0] // (batch * nj)
        assert rows * batch * nj == q_s.shape[0]
        side_in, side_out = _xattn_rows_specs(rows, cache_k.shape[2], cache_layer, step)
        in_specs += side_in
        out_specs += [side_out]
        out_shape += [jax.ShapeDtypeStruct(q_s.shape, F32)]
        args += [q_s, cache_k, cache_v]
        body = _xattn_mem_kernel
    if cast_ffn is not None:
        assert mem_step is None
        w_up, w_down, cast_layer = cast_ffn
        wup_in, wup_out, wup_shape = _cast_specs(w_up, cast_layer, batch * nj, step)
        wdn_in, wdn_out, wdn_shape = _cast_specs(w_down, cast_layer, batch * nj, step)
        in_specs += [wup_in, wdn_in]
        out_specs += [wup_out, wdn_out]
        out_shape += [wup_shape, wdn_shape]
        args += [w_up, w_down]
        body = _xattn_cast_kernel
    return pl.pallas_call(
        body,
        grid=(batch, nj),
        in_specs=in_specs,
        out_specs=out_specs,
        out_shape=out_shape,
        scratch_shapes=[pltpu.VMEM((tm, D_MODEL), BF16)],
        compiler_params=_params("arbitrary", "arbitrary"),
        name="xattn",
    )(*args)


def _ffn_tail(x, y_scr, wdn_ref, gf_ref, xo_ref, final):
    out = x + _dot(y_scr[...], wdn_ref[...])
    if final:
        out = _rms(out, gf_ref[...])
    xo_ref[...] = out


def _ffn_mem_kernel(x_ref, g_ref, wup_ref, cw_ref, cb_ref, wdn_ref, gf_ref, qs_ref, ck_ref, cv_ref,
                    xo_ref, buf_ref, os_ref, carry_scr, y_scr, *, final):
    def per_chunk(n):
        if n < qs_ref.shape[0]:
            _xattn_row(qs_ref, ck_ref, cv_ref, os_ref, n)

    _ffn_kernel(x_ref, g_ref, wup_ref, cw_ref, cb_ref, wdn_ref, gf_ref, xo_ref, buf_ref, carry_scr, y_scr,
                final=final, per_chunk=per_chunk)


def _ffn_kernel(x_ref, g_ref, wup_ref, cw_ref, cb_ref, wdn_ref, gf_ref, xo_ref, buf_ref,
                carry_scr, y_scr, *, final, per_chunk=None):
    @pl.when(pl.program_id(1) == 0)
    def _():
        carry_scr[...] = jnp.zeros_like(carry_scr)

    x = x_ref[...]
    tm = x.shape[0]
    hn = _rms(x, g_ref[...]).astype(BF16)
    for n in range(FFN_DIM // FFN_COLS):
        cols = slice(n * FFN_COLS, (n + 1) * FFN_COLS)
        gcols = slice(FFN_DIM + n * FFN_COLS, FFN_DIM + (n + 1) * FFN_COLS)
        u = _dot(hn, wup_ref[:, cols])
        gate = _dot(hn, wup_ref[:, gcols])
        prev = carry_scr[:, cols]
        uc = cb_ref[:, cols] + _shift_rows(u, prev, 2) * cw_ref[0:1, cols]
        uc = uc + _shift_rows(u, prev, 1) * cw_ref[1:2, cols]
        uc = uc + u * cw_ref[2:3, cols]
        y_scr[:, cols] = (_gelu(uc) * gate).astype(BF16)
        carry_scr[:, cols] = u[tm - SUBLANES:, :]
        buf_ref[:, cols] = u[tm - (FFN_CONV - 1):, :]
        if per_chunk is not None:
            per_chunk(n)
    _ffn_tail(x, y_scr, wdn_ref, gf_ref, xo_ref, final)


def _ffn(x, g, wup, cw, cb, wdn, gf, batch, seq, tm, final, mem_step=None):
    t = x.shape[0]
    nj = seq // tm
    step = lambda b, j: b * nj + j
    side_in, side_out, side_shape, side_args, body = [], [], [], [], _ffn_kernel
    if mem_step is not None:
        q_s, cache_k, cache_v, cache_layer = mem_step
        rows = q_s.shape[0] // (batch * nj)
        assert rows * batch * nj == q_s.shape[0] and rows <= FFN_DIM // FFN_COLS
        side_in, out = _xattn_rows_specs(rows, cache_k.shape[2], cache_layer, step)
        side_out, side_shape, side_args = [out], [jax.ShapeDtypeStruct(q_s.shape, F32)], [q_s, cache_k, cache_v]
        body = _ffn_mem_kernel
    return pl.pallas_call(
        functools.partial(body, final=final),
        grid=(batch, nj),
        in_specs=[
            pl.BlockSpec((tm, D_MODEL), lambda b, j: (step(b, j), 0)),
            _resident((1, D_MODEL)),
            _resident((D_MODEL, 2 * FFN_DIM)),
            _resident((FFN_CONV, FFN_DIM)),
            _resident((1, FFN_DIM)),
            _resident((FFN_DIM, D_MODEL)),
            _resident((1, D_MODEL)),
        ] + side_in,
        out_specs=[
            pl.BlockSpec((tm, D_MODEL), lambda b, j: (step(b, j), 0)),
            pl.BlockSpec((None, FFN_CONV - 1, FFN_DIM), lambda b, j: (b, 0, 0)),
        ] + side_out,
        out_shape=[
            jax.ShapeDtypeStruct((t, D_MODEL), F32),
            jax.ShapeDtypeStruct((batch, FFN_CONV - 1, FFN_DIM), F32),
        ] + side_shape,
        scratch_shapes=[pltpu.VMEM((SUBLANES, FFN_DIM), F32), pltpu.VMEM((tm, FFN_DIM), BF16)],
        compiler_params=_params("arbitrary", "arbitrary"),
        name="ffn",
    )(x, g, wup, cw, cb, wdn, gf, *side_args)


def _ffn_step_kernel(x_ref, g_ref, wup_ref, cw_ref, cb_ref, wdn_ref, gf_ref, buf_ref, xo_ref, nbuf_ref,
                     y_scr, *, final):
    x = x_ref[...]
    hn = _rms(x, g_ref[...]).astype(BF16)
    for n in range(FFN_DIM // FFN_COLS):
        cols = slice(n * FFN_COLS, (n + 1) * FFN_COLS)
        gcols = slice(FFN_DIM + n * FFN_COLS, FFN_DIM + (n + 1) * FFN_COLS)
        u = _dot(hn, wup_ref[:, cols])
        gate = _dot(hn, wup_ref[:, gcols])
        b0 = buf_ref[:, cols]
        b1 = buf_ref[:, gcols]
        uc = cb_ref[:, cols] + b0 * cw_ref[0:1, cols]
        uc = uc + b1 * cw_ref[1:2, cols]
        uc = uc + u * cw_ref[2:3, cols]
        y_scr[:, cols] = (_gelu(uc) * gate).astype(BF16)
        nbuf_ref[:, cols] = b1
        nbuf_ref[:, gcols] = u
    _ffn_tail(x, y_scr, wdn_ref, gf_ref, xo_ref, final)


def _ffn_step(x, g, wup, cw, cb, wdn, gf, buf, final):
    t = x.shape[0]
    return pl.pallas_call(
        functools.partial(_ffn_step_kernel, final=final),
        out_shape=[
            jax.ShapeDtypeStruct((t, D_MODEL), F32),
            jax.ShapeDtypeStruct(buf.shape, F32),
        ],
        scratch_shapes=[pltpu.VMEM((t, FFN_DIM), BF16)],
        compiler_params=pltpu.CompilerParams(vmem_limit_bytes=VMEM_LIMIT),
        name="ffn_step",
    )(x, g, wup, cw, cb, wdn, gf, buf)


def _lru_gates(xc, wa_ref, ba_ref, wx_ref, bx_ref, lam_ref, a_out, u_out):
    xcb = xc.astype(BF16)
    log_s = _log_sigmoid(lam_ref[...])
    for blk in range(LRU_BLOCKS):
        cols = slice(blk * LRU_BW, (blk + 1) * LRU_BW)
        r = jax.nn.sigmoid(_dot(xcb[:, cols], wa_ref[blk]) + ba_ref[:, cols])
        i = jax.nn.sigmoid(_dot(xcb[:, cols], wx_ref[blk]) + bx_ref[:, cols])
        log_a = LRU_C * r * log_s[:, cols]
        a = jnp.exp(log_a)
        a_out[:, cols] = a
        u_out[:, cols] = jnp.sqrt(-jnp.tanh(log_a) * (a * a + 1.0)) * (i * xc[:, cols])


def _lru_kernel(x_ref, g_ref, win_ref, cw_ref, cb_ref, wa_ref, ba_ref, wx_ref, bx_ref, lam_ref, wout_ref,
                q_ref, k_ref, v_ref, gt_ref, s_ref, xo_ref, h_ref, buf_ref, so_ref, y_ref,
                carry_scr, h_scr, a_scr, u_scr):
    @pl.when(pl.program_id(1) == 0)
    def _():
        carry_scr[...] = jnp.zeros_like(carry_scr)
        h_scr[...] = jnp.zeros_like(h_scr)

    x = x_ref[...]
    tm = x.shape[0]
    hn = _rms(x, g_ref[...]).astype(BF16)
    xb = _dot(hn, win_ref[:, :LRU_WIDTH])
    prev = carry_scr[...]
    xc = cb_ref[...] + _shift_rows(xb, prev, 3) * cw_ref[0:1, :]
    xc = xc + _shift_rows(xb, prev, 2) * cw_ref[1:2, :]
    xc = xc + _shift_rows(xb, prev, 1) * cw_ref[2:3, :]
    xc = xc + xb * cw_ref[3:4, :]
    carry_scr[...] = xb[tm - SUBLANES:, :]
    buf_ref[...] = xb[tm - (LRU_CONV - 1):, :]
    _lru_gates(xc, wa_ref, ba_ref, wx_ref, bx_ref, lam_ref, a_scr, u_scr)

    sub = lax.broadcasted_iota(jnp.int32, (SUBLANES, LRU_WIDTH), 0)

    h = h_scr[...]
    for gi in range(tm // SUBLANES):
        rows = slice(gi * SUBLANES, (gi + 1) * SUBLANES)
        a = a_scr[rows, :]
        u = u_scr[rows, :]
        for s in (1, 2, 4):
            keep = sub >= s
            u = jnp.where(keep, a * pltpu.roll(u, s, 0) + u, u)
            a = jnp.where(keep, a * pltpu.roll(a, s, 0), a)
        hs = a * h + u
        u_scr[rows, :] = hs
        h = hs[SUBLANES - 1:, :]
    h_scr[...] = h
    h_ref[...] = h
    gb = _dot(hn, win_ref[:, LRU_WIDTH:])
    y = (_gelu(gb) * u_scr[...]).astype(BF16)
    xo_ref[...] = x + _dot(y, wout_ref[...])
    _ret_rows(q_ref, k_ref, v_ref, gt_ref, s_ref, so_ref, y_ref)


def _lru_weight_specs():
    return [
        _resident((1, D_MODEL)),
        _resident((D_MODEL, 2 * LRU_WIDTH)),
        _resident((LRU_CONV, LRU_WIDTH)),
        _resident((1, LRU_WIDTH)),
        _resident((LRU_BLOCKS, LRU_BW, LRU_BW)),
        _resident((1, LRU_WIDTH)),
        _resident((LRU_BLOCKS, LRU_BW, LRU_BW)),
        _resident((1, LRU_WIDTH)),
        _resident((1, LRU_WIDTH)),
        _resident((LRU_WIDTH, D_MODEL)),
    ]


def _lru(x, weights, qkvg_s, state, batch, seq, tm):
    t = x.shape[0]
    nj = seq // tm
    steps, rows, _ = qkvg_s.shape
    assert steps == batch * nj and steps * rows == state.shape[1]
    ret_in, ret_out = _ret_rows_specs(rows, lambda b, j: b * nj + j)
    return pl.pallas_call(
        _lru_kernel,
        grid=(batch, nj),
        in_specs=[pl.BlockSpec((tm, D_MODEL), lambda b, j: (b * nj + j, 0))] + _lru_weight_specs() + ret_in,
        out_specs=[
            pl.BlockSpec((tm, D_MODEL), lambda b, j: (b * nj + j, 0)),
            pl.BlockSpec((None, 1, LRU_WIDTH), lambda b, j: (b, 0, 0)),
            pl.BlockSpec((None, LRU_CONV - 1, LRU_WIDTH), lambda b, j: (b, 0, 0)),
        ] + ret_out,
        out_shape=[
            jax.ShapeDtypeStruct((t, D_MODEL), F32),
            jax.ShapeDtypeStruct((batch, 1, LRU_WIDTH), F32),
            jax.ShapeDtypeStruct((batch, LRU_CONV - 1, LRU_WIDTH), F32),
            jax.ShapeDtypeStruct(state.shape, F32),
            jax.ShapeDtypeStruct((steps, rows, RET_V), F32),
        ],
        scratch_shapes=[
            pltpu.VMEM((SUBLANES, LRU_WIDTH), F32),
            pltpu.VMEM((1, LRU_WIDTH), F32),
            pltpu.VMEM((tm, LRU_WIDTH), F32),
            pltpu.VMEM((tm, LRU_WIDTH), F32),
        ],
        compiler_params=_params("arbitrary", "arbitrary"),
        name="lru",
    )(x, *weights, qkvg_s, qkvg_s, qkvg_s, qkvg_s, state)


def _lru_step_kernel(x_ref, g_ref, win_ref, cw_ref, cb_ref, wa_ref, ba_ref, wx_ref, bx_ref, lam_ref, wout_ref,
                     h0_ref, buf_ref, xo_ref, h_ref, nbuf_ref, a_scr, u_scr):
    x = x_ref[...]
    w = LRU_WIDTH
    hn = _rms(x, g_ref[...]).astype(BF16)
    xb = _dot(hn, win_ref[:, :w])
    xc = cb_ref[...] + buf_ref[:, 0:w] * cw_ref[0:1, :]
    xc = xc + buf_ref[:, w:2 * w] * cw_ref[1:2, :]
    xc = xc + buf_ref[:, 2 * w:3 * w] * cw_ref[2:3, :]
    xc = xc + xb * cw_ref[3:4, :]
    nbuf_ref[:, 0:2 * w] = buf_ref[:, w:3 * w]
    nbuf_ref[:, 2 * w:3 * w] = xb
    _lru_gates(xc, wa_ref, ba_ref, wx_ref, bx_ref, lam_ref, a_scr, u_scr)
    hs = a_scr[...] * h0_ref[...] + u_scr[...]
    h_ref[...] = hs
    gb = _dot(hn, win_ref[:, w:])
    xo_ref[...] = x + _dot((_gelu(gb) * hs).astype(BF16), wout_ref[...])


def _lru_step(x, weights, h0, buf):
    t = x.shape[0]
    return pl.pallas_call(
        _lru_step_kernel,
        out_shape=[
            jax.ShapeDtypeStruct((t, D_MODEL), F32),
            jax.ShapeDtypeStruct(h0.shape, F32),
            jax.ShapeDtypeStruct(buf.shape, F32),
        ],
        scratch_shapes=[pltpu.VMEM((t, LRU_WIDTH), F32), pltpu.VMEM((t, LRU_WIDTH), F32)],
        compiler_params=pltpu.CompilerParams(vmem_limit_bytes=VMEM_LIMIT),
        name="lru_step",
    )(x, *weights, h0, buf)


def _rope_tables(positions):
    inv = ROPE_BASE ** (-jnp.arange(ROPE_HALF, dtype=F32) / ROPE_HALF)
    ang = positions[:, None] * inv[None, :]
    return jnp.cos(ang), jnp.sin(ang)


def kernel(x_prompt, x_sample, state_ret, state_lru_h, state_lru_conv, state_ffn_conv, cache_mem_k, cache_mem_v, mem_prompt, norm_mix, norm_xa, norm_mem, norm_ffn, norm_final, ret_w_in, ret_w_out, lru_w_in, lru_conv_w, lru_conv_b, lru_wa, lru_ba, lru_wx, lru_bx, lru_lambda, lru_w_out, xa_w_q, xa_w_kv, xa_w_o, ffn_w_up, ffn_conv_w, ffn_conv_b, ffn_w_down):
    bp, seq, d = x_prompt.shape
    bs = x_sample.shape[0]
    mem_len = mem_prompt.shape[1]
    assert d == D_MODEL and x_sample.shape[1] == 1
    assert seq % WIDE_ROW_TILE == 0 and WIDE_ROW_TILE % ROW_TILE == 0
    assert ROW_TILE % SCAN_CHUNK == 0

    row = lambda v: v.reshape(1, -1)
    ret_w_in_b = ret_w_in[0].astype(BF16)
    ret_w_out_b = ret_w_out[0].astype(BF16)
    xa_w_q_b = xa_w_q.astype(BF16)
    xa_w_kv_b = xa_w_kv.astype(BF16)
    xa_w_o_b = xa_w_o.astype(BF16)
    lru_weights = lambda j: (
        row(norm_mix[1]), lru_w_in[j].astype(BF16), lru_conv_w[j], row(lru_conv_b[j]),
        lru_wa[j].astype(BF16), row(lru_ba[j]), lru_wx[j].astype(BF16), row(lru_bx[j]),
        row(lru_lambda[j]), lru_w_out[j].astype(BF16))
    gf = row(norm_final)

    steps = bp * (seq // ROW_TILE)
    heads = lambda v: v.reshape(bs, XA_HEADS, XA_HD)
    ffn_args = lambda i, w_up_b, w_down_b: (
        row(norm_ffn[i]), w_up_b, ffn_conv_w[i], row(ffn_conv_b[i]), w_down_b, gf)

    mem_k, mem_v, mem_k_heads, mem_v_heads = _mem_kv(
        mem_prompt.reshape(bp * mem_len, d), norm_mem.reshape(DEPTH, 1, d), xa_w_kv_b, ROW_TILE)
    mem_k = mem_k.reshape(DEPTH, bp, mem_len, d)
    mem_v = mem_v.reshape(DEPTH, bp, mem_len, d)
    cos_p, sin_p = _rope_tables(jnp.arange(seq, dtype=F32))
    cos_s, sin_s = _rope_tables(jnp.full((bs,), PAST_LEN, F32))
    xp = x_prompt.reshape(bp * seq, d)
    xs = x_sample.reshape(bs, d)

    qkvg = _ret_proj(xp, row(norm_mix[0]), ret_w_in_b, cos_p, sin_p, WIDE_ROW_TILE, BF16)
    qkvg_s = _ret_proj(xs, row(norm_mix[0]), ret_w_in_b, cos_s, sin_s, bs, F32).reshape(steps, bs // steps, RET_COLS)
    xp, ret_p, w_up0, w_down0 = _ret_scan(qkvg, xp, ret_w_out_b, ffn_w_up, ffn_w_down, bp, seq, ROW_TILE)
    xp, w_up1, w_down1 = _xattn(xp, row(norm_xa[0]), xa_w_q_b[0], mem_k, mem_v, xa_w_o_b[0], 0, bp, seq, WIDE_ROW_TILE,
                                cast_ffn=(ffn_w_up, ffn_w_down, 1))
    xp, ffn_p0 = _ffn(xp, *ffn_args(0, w_up0, w_down0), bp, seq, WIDE_ROW_TILE, False)

    xp, lru_h_p, lru_conv_p, ret_s, y_s = _lru(xp, lru_weights(0), qkvg_s, state_ret, bp, seq, ROW_TILE)
    xs = _matmul_res(y_s.reshape(bs, RET_V), ret_w_out_b, xs)
    q_s = heads(_norm_matmul(xs, row(norm_xa[0]), xa_w_q_b[0]))
    xp, o_s = _xattn(xp, row(norm_xa[1]), xa_w_q_b[1], mem_k, mem_v, xa_w_o_b[1], 1, bp, seq, ROW_TILE,
                     mem_step=(q_s, cache_mem_k, cache_mem_v, 0))
    xs = _matmul_res(o_s.reshape(bs, d), xa_w_o_b[0], xs)
    xs, ffn_s0 = _ffn_step(xs, *ffn_args(0, w_up0, w_down0), state_ffn_conv[0].reshape(bs, -1), False)
    xs, lru_h_s, lru_conv_s = _lru_step(xs, lru_weights(0), state_lru_h[0], state_lru_conv[0].reshape(bs, -1))
    q_s = heads(_norm_matmul(xs, row(norm_xa[1]), xa_w_q_b[1]))
    xp, ffn_p1, o_s = _ffn(xp, *ffn_args(1, w_up1, w_down1), bp, seq, ROW_TILE, True,
                           mem_step=(q_s, cache_mem_k, cache_mem_v, 1))
    xs = _matmul_res(o_s.reshape(bs, d), xa_w_o_b[1], xs)
    xs, ffn_s1 = _ffn_step(xs, *ffn_args(1, w_up1, w_down1), state_ffn_conv[1].reshape(bs, -1), True)

    mem_shape = (DEPTH, bp, mem_len, XA_HEADS, XA_HD)
    return (
        xp.reshape(bp, seq, d),
        xs.reshape(bs, 1, d),
        ret_p[None],
        ret_s,
        lru_h_p.reshape(1, bp, LRU_WIDTH),
        lru_h_s[None],
        lru_conv_p[None],
        lru_conv_s.reshape(1, bs, LRU_CONV - 1, LRU_WIDTH),
        jnp.stack([ffn_p0, ffn_p1]),
        jnp.stack([ffn_s0, ffn_s1]).reshape(DEPTH, bs, FFN_CONV - 1, FFN_DIM),
        mem_k_heads.reshape(mem_shape),
        mem_v_heads.reshape(mem_shape),
    )
```

```python
import functools
import math

import jax
import jax.numpy as jnp
from jax import lax
from jax.experimental import pallas as pl
from jax.experimental.pallas import tpu as pltpu

F32 = jnp.float32
BF16 = jnp.bfloat16

D_MODEL = 1024
DEPTH = 2
PAST_LEN = 16384
RET_HEADS = 4
RET_DK = D_MODEL // RET_HEADS
RET_DV = 2 * D_MODEL // RET_HEADS
ROPE_BASE = 10000.0
ROPE_HALF = RET_DK // 2
LRU_WIDTH = D_MODEL
LRU_BLOCKS = 4
LRU_BW = LRU_WIDTH // LRU_BLOCKS
LRU_CONV = 4
LRU_C = 8.0
XA_HEADS = 4
XA_HD = D_MODEL // XA_HEADS
FFN_DIM = 3 * D_MODEL
FFN_CONV = 3
EPS = 1e-6

RET_QK = RET_HEADS * RET_DK
RET_V = RET_HEADS * RET_DV
RET_COLS = 2 * RET_QK + 2 * RET_V
RET_GAMMA = tuple(1.0 - 2.0 ** (-5.0 - h) for h in range(RET_HEADS))
RET_LOG_G = tuple(math.log(g) for g in RET_GAMMA)
K_SCALE = RET_DK ** -0.5
XA_SCALE = XA_HD ** -0.5
SQRT_2_OVER_PI = math.sqrt(2.0 / math.pi)

SUBLANES = 8
BF16_SUBLANES = 16
VMEM_LIMIT = 56 << 20

ROW_TILE = 512
WIDE_ROW_TILE = 1024
SCAN_CHUNK = 256
FFN_COLS = 512


def _params(*sem):
    return pltpu.CompilerParams(dimension_semantics=sem, vmem_limit_bytes=VMEM_LIMIT)


def _resident(shape):
    zeros = (0,) * len(shape)
    return pl.BlockSpec(shape, lambda *_: zeros, pipeline_mode=pl.Buffered(1))


def _dot(a, b):
    return jnp.dot(a, b, preferred_element_type=F32)


def _dot_nt(a, b):
    return lax.dot_general(a, b, (((1,), (1,)), ((), ())), preferred_element_type=F32)


def _rms(x, g):
    return x * lax.rsqrt(jnp.mean(x * x, axis=-1, keepdims=True) + EPS) * g


def _gelu(x):
    return x * (0.5 * (1.0 + jnp.tanh(SQRT_2_OVER_PI * (x + 0.044715 * (x * x * x)))))


def _silu(x):
    return x * jax.nn.sigmoid(x)


def _log_sigmoid(x):
    return jnp.minimum(x, 0.0) - jnp.log(1.0 + jnp.exp(-jnp.abs(x)))


def _shift_rows(x, prev, s):
    rolled = pltpu.roll(x, s, 0)
    rows = lax.broadcasted_iota(jnp.int32, (SUBLANES, 1), 0)
    head = jnp.where(rows < s, pltpu.roll(prev, s, 0), rolled[:SUBLANES])
    return jnp.concatenate([head, rolled[SUBLANES:]], axis=0)


def _cast_specs(src, layer, steps, step):
    _, r, n = src.shape
    rows = r // steps
    assert rows * steps == r and rows % BF16_SUBLANES == 0
    return (pl.BlockSpec((None, rows, n), lambda *ids: (layer, step(*ids), 0)),
            pl.BlockSpec((rows, n), lambda *ids: (step(*ids), 0)),
            jax.ShapeDtypeStruct((r, n), BF16))


def _cast_bundle_specs(srcs, steps, step):
    ins, outs, shapes = [], [], []
    for s in srcs:
        layers, r, n = s.shape
        rows = r // steps
        assert rows * steps == r and rows % BF16_SUBLANES == 0
        spec = pl.BlockSpec((layers, rows, n), lambda *ids: (0, step(*ids), 0))
        ins.append(spec)
        outs.append(spec)
        shapes.append(jax.ShapeDtypeStruct(s.shape, BF16))
    return ins, outs, shapes


def _cast_bundle(src_refs, dst_refs):
    for s_ref, d_ref in zip(src_refs, dst_refs, strict=True):
        d_ref[...] = s_ref[...].astype(BF16)


def _ret_proj_kernel(x_ref, g_ref, w_ref, cos_ref, sin_ref, *rest):
    n_cast = len(rest) // 2
    o_ref = rest[n_cast]
    _cast_bundle(rest[:n_cast], rest[n_cast + 1:])
    hn = _rms(x_ref[...], g_ref[...]).astype(BF16)
    cos = cos_ref[...]
    sin = sin_ref[...]
    for n in range(2 * RET_HEADS):
        c0 = n * RET_DK
        acc = _dot(hn, w_ref[:, c0:c0 + RET_DK])
        x1 = acc[:, :ROPE_HALF]
        x2 = acc[:, ROPE_HALF:]
        r1 = x1 * cos - x2 * sin
        r2 = x1 * sin + x2 * cos
        if n >= RET_HEADS:
            r1 = r1 * K_SCALE
            r2 = r2 * K_SCALE
        o_ref[:, c0:c0 + ROPE_HALF] = r1.astype(o_ref.dtype)
        o_ref[:, c0 + ROPE_HALF:c0 + RET_DK] = r2.astype(o_ref.dtype)
    for n in range(2 * RET_V // RET_DV):
        c0 = 2 * RET_QK + n * RET_DV
        o_ref[:, c0:c0 + RET_DV] = _dot(hn, w_ref[:, c0:c0 + RET_DV]).astype(o_ref.dtype)


def _ret_proj(x, g, w, cos, sin, tm, out_dtype, cast=()):
    t = x.shape[0]
    period = cos.shape[0] // tm
    cast_in, cast_out, cast_shape = _cast_bundle_specs(cast, t // tm, lambda m: m)
    return pl.pallas_call(
        _ret_proj_kernel,
        grid=(t // tm,),
        in_specs=[
            pl.BlockSpec((tm, D_MODEL), lambda m: (m, 0)),
            _resident((1, D_MODEL)),
            _resident((D_MODEL, RET_COLS)),
            pl.BlockSpec((tm, ROPE_HALF), lambda m: (m % period, 0)),
            pl.BlockSpec((tm, ROPE_HALF), lambda m: (m % period, 0)),
        ] + cast_in,
        out_specs=[pl.BlockSpec((tm, RET_COLS), lambda m: (m, 0))] + cast_out,
        out_shape=[jax.ShapeDtypeStruct((t, RET_COLS), out_dtype)] + cast_shape,
        compiler_params=_params("arbitrary"),
        name="ret_proj",
    )(x, g, w, cos, sin, *cast)


def _ret_scan_kernel(q_ref, k_ref, v_ref, g_ref, x_ref, w_ref, wup_ref, wdn_ref,
                     xo_ref, s_ref, wup_b_ref, wdn_b_ref, y_scr, intra_scr):
    c = SCAN_CHUNK

    @pl.when((pl.program_id(0) == 0) & (pl.program_id(1) == 0))
    def _():
        row = lax.broadcasted_iota(jnp.int32, (c, c), 0).astype(F32)
        col = lax.broadcasted_iota(jnp.int32, (c, c), 1).astype(F32)
        rel = row - col
        for h in range(RET_HEADS):
            intra_scr[h] = jnp.where(rel >= 0, jnp.exp(RET_LOG_G[h] * jnp.maximum(rel, 0.0)), 0.0)

    @pl.when(pl.program_id(1) == 0)
    def _():
        s_ref[...] = jnp.zeros_like(s_ref)

    idx = lax.broadcasted_iota(jnp.int32, (c, 1), 0).astype(F32)
    for h in range(RET_HEADS):
        lg = RET_LOG_G[h]
        intra = intra_scr[h]
        q_dec = jnp.exp(lg * (idx + 1.0))
        k_dec = jnp.exp(lg * (c - 1.0 - idx))
        chunk_dec = math.exp(lg * c)
        qk_cols = slice(h * RET_DK, (h + 1) * RET_DK)
        v_cols = slice(h * RET_DV, (h + 1) * RET_DV)
        for ci in range(q_ref.shape[0] // c):
            rows = slice(ci * c, (ci + 1) * c)
            qc = q_ref[rows, qk_cols]
            kc = k_ref[rows, qk_cols]
            vc = v_ref[rows, v_cols]
            s = s_ref[h]
            att = _dot_nt(qc, kc) * intra
            o = _dot(att.astype(BF16), vc) + _dot((qc.astype(F32) * q_dec).astype(BF16), s.astype(BF16))
            kd_t = (kc.astype(F32) * k_dec).T.astype(BF16)
            s_ref[h] = s * chunk_dec + _dot(kd_t, vc)
            o = o * lax.rsqrt(jnp.mean(o * o, axis=-1, keepdims=True) + EPS)
            y_scr[rows, v_cols] = (_silu(g_ref[rows, v_cols].astype(F32)) * o).astype(BF16)
    xo_ref[...] = x_ref[...] + _dot(y_scr[...], w_ref[...])
    wup_b_ref[...] = wup_ref[...].astype(BF16)
    wdn_b_ref[...] = wdn_ref[...].astype(BF16)


def _ret_scan(qkvg, x, w_out, ffn_w_up, ffn_w_down, batch, seq, tm):
    t = x.shape[0]
    nj = seq // tm
    step = lambda b, j: b * nj + j
    wup_in, wup_out, wup_shape = _cast_specs(ffn_w_up, 0, batch * nj, step)
    wdn_in, wdn_out, wdn_shape = _cast_specs(ffn_w_down, 0, batch * nj, step)
    return pl.pallas_call(
        _ret_scan_kernel,
        grid=(batch, nj),
        in_specs=[
            pl.BlockSpec((tm, RET_QK), lambda b, j: (b * nj + j, 0)),
            pl.BlockSpec((tm, RET_QK), lambda b, j: (b * nj + j, 1)),
            pl.BlockSpec((tm, RET_V), lambda b, j: (b * nj + j, 1)),
            pl.BlockSpec((tm, RET_V), lambda b, j: (b * nj + j, 2)),
            pl.BlockSpec((tm, D_MODEL), lambda b, j: (b * nj + j, 0)),
            _resident((RET_V, D_MODEL)),
            wup_in,
            wdn_in,
        ],
        out_specs=[
            pl.BlockSpec((tm, D_MODEL), lambda b, j: (b * nj + j, 0)),
            pl.BlockSpec((None, RET_HEADS, RET_DK, RET_DV), lambda b, j: (b, 0, 0, 0)),
            wup_out,
            wdn_out,
        ],
        out_shape=[
            jax.ShapeDtypeStruct((t, D_MODEL), F32),
            jax.ShapeDtypeStruct((batch, RET_HEADS, RET_DK, RET_DV), F32),
            wup_shape,
            wdn_shape,
        ],
        scratch_shapes=[
            pltpu.VMEM((tm, RET_V), BF16),
            pltpu.VMEM((RET_HEADS, SCAN_CHUNK, SCAN_CHUNK), F32),
        ],
        compiler_params=_params("arbitrary", "arbitrary"),
        name="ret_scan",
    )(qkvg, qkvg, qkvg, qkvg, x, w_out, ffn_w_up, ffn_w_down)


def _ret_rows(q_ref, k_ref, v_ref, g_ref, s_ref, so_ref, y_ref):
    n = q_ref.shape[0]
    pad = jnp.zeros((SUBLANES - n, RET_DK), F32)
    v = v_ref[...]
    gate = _silu(g_ref[...])
    for h in range(RET_HEADS):
        qk_cols = slice(h * RET_DK, (h + 1) * RET_DK)
        v_cols = slice(h * RET_DV, (h + 1) * RET_DV)
        q_t = jnp.concatenate([q_ref[:, qk_cols], pad], axis=0).T
        k_t = jnp.concatenate([k_ref[:, qk_cols], pad], axis=0).T
        for i in range(n):
            s_new = s_ref[i, h] * RET_GAMMA[h] + k_t[:, i:i + 1] * v[i:i + 1, v_cols]
            so_ref[i, h] = s_new
            o = jnp.sum(q_t[:, i:i + 1] * s_new, axis=0, keepdims=True)
            o = o * lax.rsqrt(jnp.mean(o * o, axis=-1, keepdims=True) + EPS)
            y_ref[i:i + 1, v_cols] = gate[i:i + 1, v_cols] * o


def _ret_rows_specs(rows, step):
    state = pl.BlockSpec((None, rows, RET_HEADS, RET_DK, RET_DV), lambda *ids: (0, step(*ids), 0, 0, 0))
    ins = [
        pl.BlockSpec((None, rows, RET_QK), lambda *ids: (step(*ids), 0, 0)),
        pl.BlockSpec((None, rows, RET_QK), lambda *ids: (step(*ids), 0, 1)),
        pl.BlockSpec((None, rows, RET_V), lambda *ids: (step(*ids), 0, 1)),
        pl.BlockSpec((None, rows, RET_V), lambda *ids: (step(*ids), 0, 2)),
        state,
    ]
    outs = [state, pl.BlockSpec((None, rows, RET_V), lambda *ids: (step(*ids), 0, 0))]
    return ins, outs


def _norm_matmul_kernel(x_ref, g_ref, w_ref, o_ref):
    o_ref[...] = _dot(_rms(x_ref[...], g_ref[...]).astype(BF16), w_ref[...])


def _norm_matmul(x, g, w):
    return pl.pallas_call(
        _norm_matmul_kernel,
        out_shape=jax.ShapeDtypeStruct((x.shape[0], w.shape[1]), F32),
        compiler_params=pltpu.CompilerParams(vmem_limit_bytes=VMEM_LIMIT),
        name="norm_matmul",
    )(x, g, w)


def _matmul_res_kernel(y_ref, w_ref, x_ref, o_ref):
    o_ref[...] = x_ref[...] + _dot(y_ref[...].astype(BF16), w_ref[...])


def _matmul_res(y, w, x):
    return pl.pallas_call(
        _matmul_res_kernel,
        out_shape=jax.ShapeDtypeStruct(x.shape, F32),
        compiler_params=pltpu.CompilerParams(vmem_limit_bytes=VMEM_LIMIT),
        name="matmul_res",
    )(y, w, x)


def _mem_kv_kernel(m_ref, g_ref, w_ref, wret_ref, k_ref, v_ref, kh_ref, vh_ref, wret_b_ref, w_scr):
    @pl.when(pl.program_id(1) == 0)
    def _():
        w_scr[...] = w_ref[...].astype(BF16)

    wret_b_ref[...] = wret_ref[...].astype(BF16)
    hn = _rms(m_ref[...], g_ref[...]).astype(BF16)
    k = _dot(hn, w_scr[:, :D_MODEL])
    v = _dot(hn, w_scr[:, D_MODEL:])
    k_ref[...] = k
    v_ref[...] = v
    kh_ref[...] = k.reshape(kh_ref.shape)
    vh_ref[...] = v.reshape(vh_ref.shape)


def _mem_kv(mem, g, w, ret_w_in, tm):
    t = mem.shape[0]
    nm = t // tm
    wret_in, wret_out, wret_shape = _cast_specs(ret_w_in, 0, DEPTH * nm, lambda i, m: i * nm + m)
    out = jax.ShapeDtypeStruct((DEPTH, t, D_MODEL), F32)
    out_heads = jax.ShapeDtypeStruct((DEPTH, t, XA_HEADS, XA_HD), F32)
    heads_spec = pl.BlockSpec((None, tm, XA_HEADS, XA_HD), lambda i, m: (i, m, 0, 0))
    return pl.pallas_call(
        _mem_kv_kernel,
        grid=(DEPTH, t // tm),
        in_specs=[
            pl.BlockSpec((tm, D_MODEL), lambda i, m: (m, 0)),
            pl.BlockSpec((None, 1, D_MODEL), lambda i, m: (i, 0, 0)),
            pl.BlockSpec((None, D_MODEL, 2 * D_MODEL), lambda i, m: (i, 0, 0)),
            wret_in,
        ],
        out_specs=[
            pl.BlockSpec((None, tm, D_MODEL), lambda i, m: (i, m, 0)),
            pl.BlockSpec((None, tm, D_MODEL), lambda i, m: (i, m, 0)),
            heads_spec,
            heads_spec,
            wret_out,
        ],
        out_shape=[out, out, out_heads, out_heads, wret_shape],
        scratch_shapes=[pltpu.VMEM((D_MODEL, 2 * D_MODEL), BF16)],
        compiler_params=_params("arbitrary", "arbitrary"),
        name="mem_kv",
    )(mem, g, w, ret_w_in)


def _xattn_rows_specs(rows, mem_len, layer, step):
    row = pl.BlockSpec((rows, XA_HEADS, XA_HD), lambda *ids: (step(*ids), 0, 0))
    mem = pl.BlockSpec((None, rows, mem_len, XA_HEADS, XA_HD), lambda *ids: (layer, step(*ids), 0, 0, 0))
    return [row, mem, mem], row


def _xattn_row(q_ref, mk_ref, mv_ref, o_ref, i):
    mem = mk_ref.shape[1]
    k2 = mk_ref[i].reshape(mem // 2, 2 * XA_HEADS, XA_HD)
    v2 = mv_ref[i].reshape(mem // 2, 2 * XA_HEADS, XA_HD)
    q = q_ref[i]
    s = jnp.sum(k2 * jnp.concatenate([q, q], axis=0)[None], axis=-1, keepdims=True) * XA_SCALE
    m = jnp.max(s, axis=0, keepdims=True)
    e = jnp.exp(s - jnp.maximum(m, pltpu.roll(m, XA_HEADS, 1)))
    l = jnp.sum(e, axis=0, keepdims=True)
    p = e / (l + pltpu.roll(l, XA_HEADS, 1))
    o2 = jnp.sum(p * v2, axis=0)
    o_ref[i] = o2[:XA_HEADS] + o2[XA_HEADS:]


def _xattn_main(x_ref, g_ref, wq_ref, mk_ref, mv_ref, wo_ref, xo_ref, o_scr, per_head=None):
    x = x_ref[...]
    q = _dot(_rms(x, g_ref[...]).astype(BF16), wq_ref[...])
    for h in range(XA_HEADS):
        cols = slice(h * XA_HD, (h + 1) * XA_HD)
        s = _dot_nt(q[:, cols].astype(BF16), mk_ref[:, cols].astype(BF16)) * XA_SCALE
        e = jnp.exp(s - jnp.max(s, axis=-1, keepdims=True))
        p = e / jnp.sum(e, axis=-1, keepdims=True)
        o_scr[:, cols] = _dot(p.astype(BF16), mv_ref[:, cols].astype(BF16)).astype(BF16)
        if per_head is not None:
            per_head(h)
    xo_ref[...] = x + _dot(o_scr[...], wo_ref[...])


def _xattn_kernel(x_ref, g_ref, wq_ref, mk_ref, mv_ref, wo_ref, xo_ref, o_scr):
    _xattn_main(x_ref, g_ref, wq_ref, mk_ref, mv_ref, wo_ref, xo_ref, o_scr)


def _xattn_cast_kernel(x_ref, g_ref, wq_ref, mk_ref, mv_ref, wo_ref, wup_ref, wdn_ref,
                       xo_ref, wup_b_ref, wdn_b_ref, o_scr):
    _xattn_main(x_ref, g_ref, wq_ref, mk_ref, mv_ref, wo_ref, xo_ref, o_scr)
    wup_b_ref[...] = wup_ref[...].astype(BF16)
    wdn_b_ref[...] = wdn_ref[...].astype(BF16)


def _xattn_mem_kernel(x_ref, g_ref, wq_ref, mk_ref, mv_ref, wo_ref, qs_ref, ck_ref, cv_ref,
                      xo_ref, os_ref, o_scr):
    rows = qs_ref.shape[0]
    assert rows <= XA_HEADS

    def per_head(h):
        if h < rows:
            _xattn_row(qs_ref, ck_ref, cv_ref, os_ref, h)

    _xattn_main(x_ref, g_ref, wq_ref, mk_ref, mv_ref, wo_ref, xo_ref, o_scr, per_head)


def _xattn(x, g, wq, mk, mv, wo, layer, batch, seq, tm, mem_step=None, cast_ffn=None):
    t = x.shape[0]
    nj = seq // tm
    mem_len = mk.shape[2]
    step = lambda b, j: b * nj + j
    mem_spec = pl.BlockSpec((None, None, mem_len, D_MODEL), lambda b, j: (layer, b, 0, 0))
    row_spec = pl.BlockSpec((tm, D_MODEL), lambda b, j: (step(b, j), 0))
    in_specs = [row_spec, _resident((1, D_MODEL)), _resident((D_MODEL, D_MODEL)), mem_spec, mem_spec,
                _resident((D_MODEL, D_MODEL))]
    out_specs = [row_spec]
    out_shape = [jax.ShapeDtypeStruct((t, D_MODEL), F32)]
    args = [x, g, wq, mk, mv, wo]
    body = _xattn_kernel
    if mem_step is not None:
        q_s, cache_k, cache_v, cache_layer = mem_step
        rows = q_s.shape[0] // (batch * nj)
        assert rows * batch * nj == q_s.shape[0]
        side_in, side_out = _xattn_rows_specs(rows, cache_k.shape[2], cache_layer, step)
        in_specs += side_in
        out_specs += [side_out]
        out_shape += [jax.ShapeDtypeStruct(q_s.shape, F32)]
        args += [q_s, cache_k, cache_v]
        body = _xattn_mem_kernel
    if cast_ffn is not None:
        assert mem_step is None
        w_up, w_down, cast_layer = cast_ffn
        wup_in, wup_out, wup_shape = _cast_specs(w_up, cast_layer, batch * nj, step)
        wdn_in, wdn_out, wdn_shape = _cast_specs(w_down, cast_layer, batch * nj, step)
        in_specs += [wup_in, wdn_in]
        out_specs += [wup_out, wdn_out]
        out_shape += [wup_shape, wdn_shape]
        args += [w_up, w_down]
        body = _xattn_cast_kernel
    return pl.pallas_call(
        body,
        grid=(batch, nj),
        in_specs=in_specs,
        out_specs=out_specs,
        out_shape=out_shape,
        scratch_shapes=[pltpu.VMEM((tm, D_MODEL), BF16)],
        compiler_params=_params("arbitrary", "arbitrary"),
        name="xattn",
    )(*args)


def _ffn_tail(x, y_scr, wdn_ref, gf_ref, xo_ref, final):
    out = x + _dot(y_scr[...], wdn_ref[...])
    if final:
        out = _rms(out, gf_ref[...])
    xo_ref[...] = out


def _ffn_mem_kernel(x_ref, g_ref, wup_ref, cw_ref, cb_ref, wdn_ref, gf_ref, qs_ref, ck_ref, cv_ref,
                    xo_ref, buf_ref, os_ref, carry_scr, y_scr, *, final):
    def per_chunk(n):
        if n < qs_ref.shape[0]:
            _xattn_row(qs_ref, ck_ref, cv_ref, os_ref, n)

    _ffn_kernel(x_ref, g_ref, wup_ref, cw_ref, cb_ref, wdn_ref, gf_ref, xo_ref, buf_ref, carry_scr, y_scr,
                final=final, per_chunk=per_chunk)


def _ffn_kernel(x_ref, g_ref, wup_ref, cw_ref, cb_ref, wdn_ref, gf_ref, xo_ref, buf_ref,
                carry_scr, y_scr, *, final, per_chunk=None):
    @pl.when(pl.program_id(1) == 0)
    def _():
        carry_scr[...] = jnp.zeros_like(carry_scr)

    x = x_ref[...]
    tm = x.shape[0]
    hn = _rms(x, g_ref[...]).astype(BF16)
    for n in range(FFN_DIM // FFN_COLS):
        cols = slice(n * FFN_COLS, (n + 1) * FFN_COLS)
        gcols = slice(FFN_DIM + n * FFN_COLS, FFN_DIM + (n + 1) * FFN_COLS)
        u = _dot(hn, wup_ref[:, cols])
        gate = _dot(hn, wup_ref[:, gcols])
        prev = carry_scr[:, cols]
        uc = cb_ref[:, cols] + _shift_rows(u, prev, 2) * cw_ref[0:1, cols]
        uc = uc + _shift_rows(u, prev, 1) * cw_ref[1:2, cols]
        uc = uc + u * cw_ref[2:3, cols]
        y_scr[:, cols] = (_gelu(uc) * gate).astype(BF16)
        carry_scr[:, cols] = u[tm - SUBLANES:, :]
        buf_ref[:, cols] = u[tm - (FFN_CONV - 1):, :]
        if per_chunk is not None:
            per_chunk(n)
    _ffn_tail(x, y_scr, wdn_ref, gf_ref, xo_ref, final)


def _ffn(x, g, wup, cw, cb, wdn, gf, batch, seq, tm, final, mem_step=None):
    t = x.shape[0]
    nj = seq // tm
    step = lambda b, j: b * nj + j
    side_in, side_out, side_shape, side_args, body = [], [], [], [], _ffn_kernel
    if mem_step is not None:
        q_s, cache_k, cache_v, cache_layer = mem_step
        rows = q_s.shape[0] // (batch * nj)
        assert rows * batch * nj == q_s.shape[0] and rows <= FFN_DIM // FFN_COLS
        side_in, out = _xattn_rows_specs(rows, cache_k.shape[2], cache_layer, step)
        side_out, side_shape, side_args = [out], [jax.ShapeDtypeStruct(q_s.shape, F32)], [q_s, cache_k, cache_v]
        body = _ffn_mem_kernel
    return pl.pallas_call(
        functools.partial(body, final=final),
        grid=(batch, nj),
        in_specs=[
            pl.BlockSpec((tm, D_MODEL), lambda b, j: (step(b, j), 0)),
            _resident((1, D_MODEL)),
            _resident((D_MODEL, 2 * FFN_DIM)),
            _resident((FFN_CONV, FFN_DIM)),
            _resident((1, FFN_DIM)),
            _resident((FFN_DIM, D_MODEL)),
            _resident((1, D_MODEL)),
        ] + side_in,
        out_specs=[
            pl.BlockSpec((tm, D_MODEL), lambda b, j: (step(b, j), 0)),
            pl.BlockSpec((None, FFN_CONV - 1, FFN_DIM), lambda b, j: (b, 0, 0)),
        ] + side_out,
        out_shape=[
            jax.ShapeDtypeStruct((t, D_MODEL), F32),
            jax.ShapeDtypeStruct((batch, FFN_CONV - 1, FFN_DIM), F32),
        ] + side_shape,
        scratch_shapes=[pltpu.VMEM((SUBLANES, FFN_DIM), F32), pltpu.VMEM((tm, FFN_DIM), BF16)],
        compiler_params=_params("arbitrary", "arbitrary"),
        name="ffn",
    )(x, g, wup, cw, cb, wdn, gf, *side_args)


def _ffn_step_kernel(x_ref, g_ref, wup_ref, cw_ref, cb_ref, wdn_ref, gf_ref, buf_ref, xo_ref, nbuf_ref,
                     y_scr, *, final):
    x = x_ref[...]
    hn = _rms(x, g_ref[...]).astype(BF16)
    for n in range(FFN_DIM // FFN_COLS):
        cols = slice(n * FFN_COLS, (n + 1) * FFN_COLS)
        gcols = slice(FFN_DIM + n * FFN_COLS, FFN_DIM + (n + 1) * FFN_COLS)
        u = _dot(hn, wup_ref[:, cols])
        gate = _dot(hn, wup_ref[:, gcols])
        b0 = buf_ref[:, cols]
        b1 = buf_ref[:, gcols]
        uc = cb_ref[:, cols] + b0 * cw_ref[0:1, cols]
        uc = uc + b1 * cw_ref[1:2, cols]
        uc = uc + u * cw_ref[2:3, cols]
        y_scr[:, cols] = (_gelu(uc) * gate).astype(BF16)
        nbuf_ref[:, cols] = b1
        nbuf_ref[:, gcols] = u
    _ffn_tail(x, y_scr, wdn_ref, gf_ref, xo_ref, final)


def _ffn_step(x, g, wup, cw, cb, wdn, gf, buf, final):
    t = x.shape[0]
    return pl.pallas_call(
        functools.partial(_ffn_step_kernel, final=final),
        out_shape=[
            jax.ShapeDtypeStruct((t, D_MODEL), F32),
            jax.ShapeDtypeStruct(buf.shape, F32),
        ],
        scratch_shapes=[pltpu.VMEM((t, FFN_DIM), BF16)],
        compiler_params=pltpu.CompilerParams(vmem_limit_bytes=VMEM_LIMIT),
        name="ffn_step",
    )(x, g, wup, cw, cb, wdn, gf, buf)


def _lru_gates(xc, wa_ref, ba_ref, wx_ref, bx_ref, lam_ref, a_out, u_out):
    xcb = xc.astype(BF16)
    log_s = _log_sigmoid(lam_ref[...])
    for blk in range(LRU_BLOCKS):
        cols = slice(blk * LRU_BW, (blk + 1) * LRU_BW)
        r = jax.nn.sigmoid(_dot(xcb[:, cols], wa_ref[blk]) + ba_ref[:, cols])
        i = jax.nn.sigmoid(_dot(xcb[:, cols], wx_ref[blk]) + bx_ref[:, cols])
        log_a = LRU_C * r * log_s[:, cols]
        a = jnp.exp(log_a)
        a_out[:, cols] = a
        u_out[:, cols] = jnp.sqrt(-jnp.tanh(log_a) * (a * a + 1.0)) * (i * xc[:, cols])


def _lru_kernel(x_ref, g_ref, win_ref, cw_ref, cb_ref, wa_ref, ba_ref, wx_ref, bx_ref, lam_ref, wout_ref,
                q_ref, k_ref, v_ref, gt_ref, s_ref, xo_ref, h_ref, buf_ref, so_ref, y_ref,
                carry_scr, h_scr, a_scr, u_scr):
    @pl.when(pl.program_id(1) == 0)
    def _():
        carry_scr[...] = jnp.zeros_like(carry_scr)
        h_scr[...] = jnp.zeros_like(h_scr)

    x = x_ref[...]
    tm = x.shape[0]
    hn = _rms(x, g_ref[...]).astype(BF16)
    xb = _dot(hn, win_ref[:, :LRU_WIDTH])
    prev = carry_scr[...]
    xc = cb_ref[...] + _shift_rows(xb, prev, 3) * cw_ref[0:1, :]
    xc = xc + _shift_rows(xb, prev, 2) * cw_ref[1:2, :]
    xc = xc + _shift_rows(xb, prev, 1) * cw_ref[2:3, :]
    xc = xc + xb * cw_ref[3:4, :]
    carry_scr[...] = xb[tm - SUBLANES:, :]
    buf_ref[...] = xb[tm - (LRU_CONV - 1):, :]
    _lru_gates(xc, wa_ref, ba_ref, wx_ref, bx_ref, lam_ref, a_scr, u_scr)

    sub = lax.broadcasted_iota(jnp.int32, (SUBLANES, LRU_WIDTH), 0)

    h = h_scr[...]
    for gi in range(tm // SUBLANES):
        rows = slice(gi * SUBLANES, (gi + 1) * SUBLANES)
        a = a_scr[rows, :]
        u = u_scr[rows, :]
        for s in (1, 2, 4):
            keep = sub >= s
            u = jnp.where(keep, a * pltpu.roll(u, s, 0) + u, u)
            a = jnp.where(keep, a * pltpu.roll(a, s, 0), a)
        hs = a * h + u
        u_scr[rows, :] = hs
        h = hs[SUBLANES - 1:, :]
    h_scr[...] = h
    h_ref[...] = h
    gb = _dot(hn, win_ref[:, LRU_WIDTH:])
    y = (_gelu(gb) * u_scr[...]).astype(BF16)
    xo_ref[...] = x + _dot(y, wout_ref[...])
    _ret_rows(q_ref, k_ref, v_ref, gt_ref, s_ref, so_ref, y_ref)


def _lru_weight_specs():
    return [
        _resident((1, D_MODEL)),
        _resident((D_MODEL, 2 * LRU_WIDTH)),
        _resident((LRU_CONV, LRU_WIDTH)),
        _resident((1, LRU_WIDTH)),
        _resident((LRU_BLOCKS, LRU_BW, LRU_BW)),
        _resident((1, LRU_WIDTH)),
        _resident((LRU_BLOCKS, LRU_BW, LRU_BW)),
        _resident((1, LRU_WIDTH)),
        _resident((1, LRU_WIDTH)),
        _resident((LRU_WIDTH, D_MODEL)),
    ]


def _lru(x, weights, qkvg_s, state, batch, seq, tm):
    t = x.shape[0]
    nj = seq // tm
    steps, rows, _ = qkvg_s.shape
    assert steps == batch * nj and steps * rows == state.shape[1]
    ret_in, ret_out = _ret_rows_specs(rows, lambda b, j: b * nj + j)
    return pl.pallas_call(
        _lru_kernel,
        grid=(batch, nj),
        in_specs=[pl.BlockSpec((tm, D_MODEL), lambda b, j: (b * nj + j, 0))] + _lru_weight_specs() + ret_in,
        out_specs=[
            pl.BlockSpec((tm, D_MODEL), lambda b, j: (b * nj + j, 0)),
            pl.BlockSpec((None, 1, LRU_WIDTH), lambda b, j: (b, 0, 0)),
            pl.BlockSpec((None, LRU_CONV - 1, LRU_WIDTH), lambda b, j: (b, 0, 0)),
        ] + ret_out,
        out_shape=[
            jax.ShapeDtypeStruct((t, D_MODEL), F32),
            jax.ShapeDtypeStruct((batch, 1, LRU_WIDTH), F32),
            jax.ShapeDtypeStruct((batch, LRU_CONV - 1, LRU_WIDTH), F32),
            jax.ShapeDtypeStruct(state.shape, F32),
            jax.ShapeDtypeStruct((steps, rows, RET_V), F32),
        ],
        scratch_shapes=[
            pltpu.VMEM((SUBLANES, LRU_WIDTH), F32),
            pltpu.VMEM((1, LRU_WIDTH), F32),
            pltpu.VMEM((tm, LRU_WIDTH), F32),
            pltpu.VMEM((tm, LRU_WIDTH), F32),
        ],
        compiler_params=_params("arbitrary", "arbitrary"),
        name="lru",
    )(x, *weights, qkvg_s, qkvg_s, qkvg_s, qkvg_s, state)


def _lru_step_kernel(x_ref, g_ref, win_ref, cw_ref, cb_ref, wa_ref, ba_ref, wx_ref, bx_ref, lam_ref, wout_ref,
                     h0_ref, buf_ref, xo_ref, h_ref, nbuf_ref, a_scr, u_scr):
    x = x_ref[...]
    w = LRU_WIDTH
    hn = _rms(x, g_ref[...]).astype(BF16)
    xb = _dot(hn, win_ref[:, :w])
    xc = cb_ref[...] + buf_ref[:, 0:w] * cw_ref[0:1, :]
    xc = xc + buf_ref[:, w:2 * w] * cw_ref[1:2, :]
    xc = xc + buf_ref[:, 2 * w:3 * w] * cw_ref[2:3, :]
    xc = xc + xb * cw_ref[3:4, :]
    nbuf_ref[:, 0:2 * w] = buf_ref[:, w:3 * w]
    nbuf_ref[:, 2 * w:3 * w] = xb
    _lru_gates(xc, wa_ref, ba_ref, wx_ref, bx_ref, lam_ref, a_scr, u_scr)
    hs = a_scr[...] * h0_ref[...] + u_scr[...]
    h_ref[...] = hs
    gb = _dot(hn, win_ref[:, w:])
    xo_ref[...] = x + _dot((_gelu(gb) * hs).astype(BF16), wout_ref[...])


def _lru_step(x, weights, h0, buf):
    t = x.shape[0]
    return pl.pallas_call(
        _lru_step_kernel,
        out_shape=[
            jax.ShapeDtypeStruct((t, D_MODEL), F32),
            jax.ShapeDtypeStruct(h0.shape, F32),
            jax.ShapeDtypeStruct(buf.shape, F32),
        ],
        scratch_shapes=[pltpu.VMEM((t, LRU_WIDTH), F32), pltpu.VMEM((t, LRU_WIDTH), F32)],
        compiler_params=pltpu.CompilerParams(vmem_limit_bytes=VMEM_LIMIT),
        name="lru_step",
    )(x, *weights, h0, buf)


def _rope_tables(positions):
    inv = ROPE_BASE ** (-jnp.arange(ROPE_HALF, dtype=F32) / ROPE_HALF)
    ang = positions[:, None] * inv[None, :]
    return jnp.cos(ang), jnp.sin(ang)


def kernel(x_prompt, x_sample, state_ret, state_lru_h, state_lru_conv, state_ffn_conv, cache_mem_k, cache_mem_v, mem_prompt, norm_mix, norm_xa, norm_mem, norm_ffn, norm_final, ret_w_in, ret_w_out, lru_w_in, lru_conv_w, lru_conv_b, lru_wa, lru_ba, lru_wx, lru_bx, lru_lambda, lru_w_out, xa_w_q, xa_w_kv, xa_w_o, ffn_w_up, ffn_conv_w, ffn_conv_b, ffn_w_down):
    bp, seq, d = x_prompt.shape
    bs = x_sample.shape[0]
    mem_len = mem_prompt.shape[1]
    assert d == D_MODEL and x_sample.shape[1] == 1
    assert seq % WIDE_ROW_TILE == 0 and WIDE_ROW_TILE % ROW_TILE == 0
    assert ROW_TILE % SCAN_CHUNK == 0

    row = lambda v: v.reshape(1, -1)
    gf = row(norm_final)

    steps = bp * (seq // ROW_TILE)
    heads = lambda v: v.reshape(bs, XA_HEADS, XA_HD)
    ffn_args = lambda i, w_up_b, w_down_b: (
        row(norm_ffn[i]), w_up_b, ffn_conv_w[i], row(ffn_conv_b[i]), w_down_b, gf)

    mem_k, mem_v, mem_k_heads, mem_v_heads, ret_w_in_b = _mem_kv(
        mem_prompt.reshape(bp * mem_len, d), norm_mem.reshape(DEPTH, 1, d), xa_w_kv, ret_w_in, ROW_TILE)
    mem_k = mem_k.reshape(DEPTH, bp, mem_len, d)
    mem_v = mem_v.reshape(DEPTH, bp, mem_len, d)
    cos_p, sin_p = _rope_tables(jnp.arange(seq, dtype=F32))
    cos_s, sin_s = _rope_tables(jnp.full((bs,), PAST_LEN, F32))
    xp = x_prompt.reshape(bp * seq, d)
    xs = x_sample.reshape(bs, d)

    later_weights = (ret_w_out, xa_w_q, xa_w_o, lru_w_in, lru_w_out,
                     lru_wa.reshape(-1, LRU_WIDTH, LRU_BW), lru_wx.reshape(-1, LRU_WIDTH, LRU_BW))
    qkvg, ret_w_out_b, xa_w_q_b, xa_w_o_b, lru_w_in_b, lru_w_out_b, lru_wa_b, lru_wx_b = _ret_proj(
        xp, row(norm_mix[0]), ret_w_in_b, cos_p, sin_p, WIDE_ROW_TILE, BF16, cast=later_weights)
    ret_w_out_b = ret_w_out_b[0]
    lru_weights = lambda j: (
        row(norm_mix[1]), lru_w_in_b[j], lru_conv_w[j], row(lru_conv_b[j]),
        lru_wa_b.reshape(lru_wa.shape)[j], row(lru_ba[j]), lru_wx_b.reshape(lru_wx.shape)[j], row(lru_bx[j]),
        row(lru_lambda[j]), lru_w_out_b[j])
    qkvg_s, = _ret_proj(xs, row(norm_mix[0]), ret_w_in_b, cos_s, sin_s, bs, F32)
    qkvg_s = qkvg_s.reshape(steps, bs // steps, RET_COLS)
    xp, ret_p, w_up0, w_down0 = _ret_scan(qkvg, xp, ret_w_out_b, ffn_w_up, ffn_w_down, bp, seq, ROW_TILE)
    xp, w_up1, w_down1 = _xattn(xp, row(norm_xa[0]), xa_w_q_b[0], mem_k, mem_v, xa_w_o_b[0], 0, bp, seq, WIDE_ROW_TILE,
                                cast_ffn=(ffn_w_up, ffn_w_down, 1))
    xp, ffn_p0 = _ffn(xp, *ffn_args(0, w_up0, w_down0), bp, seq, WIDE_ROW_TILE, False)

    xp, lru_h_p, lru_conv_p, ret_s, y_s = _lru(xp, lru_weights(0), qkvg_s, state_ret, bp, seq, ROW_TILE)
    xs = _matmul_res(y_s.reshape(bs, RET_V), ret_w_out_b, xs)
    q_s = heads(_norm_matmul(xs, row(norm_xa[0]), xa_w_q_b[0]))
    xp, o_s = _xattn(xp, row(norm_xa[1]), xa_w_q_b[1], mem_k, mem_v, xa_w_o_b[1], 1, bp, seq, ROW_TILE,
                     mem_step=(q_s, cache_mem_k, cache_mem_v, 0))
    xs = _matmul_res(o_s.reshape(bs, d), xa_w_o_b[0], xs)
    xs, ffn_s0 = _ffn_step(xs, *ffn_args(0, w_up0, w_down0), state_ffn_conv[0].reshape(bs, -1), False)
    xs, lru_h_s, lru_conv_s = _lru_step(xs, lru_weights(0), state_lru_h[0], state_lru_conv[0].reshape(bs, -1))
    q_s = heads(_norm_matmul(xs, row(norm_xa[1]), xa_w_q_b[1]))
    xp, ffn_p1, o_s = _ffn(xp, *ffn_args(1, w_up1, w_down1), bp, seq, ROW_TILE, True,
                           mem_step=(q_s, cache_mem_k, cache_mem_v, 1))
    xs = _matmul_res(o_s.reshape(bs, d), xa_w_o_b[1], xs)
    xs, ffn_s1 = _ffn_step(xs, *ffn_args(1, w_up1, w_down1), state_ffn_conv[1].reshape(bs, -1), True)

    mem_shape = (DEPTH, bp, mem_len, XA_HEADS, XA_HD)
    return (
        xp.reshape(bp, seq, d),
        xs.reshape(bs, 1, d),
        ret_p[None],
        ret_s,
        lru_h_p.reshape(1, bp, LRU_WIDTH),
        lru_h_s[None],
        lru_conv_p[None],
        lru_conv_s.reshape(1, bs, LRU_CONV - 1, LRU_WIDTH),
        jnp.stack([ffn_p0, ffn_p1]),
        jnp.stack([ffn_s0, ffn_s1]).reshape(DEPTH, bs, FFN_CONV - 1, FFN_DIM),
        mem_k_heads.reshape(mem_shape),
        mem_v_heads.reshape(mem_shape),
    )
```

```python
import functools
import math

import jax
import jax.numpy as jnp
from jax import lax
from jax.experimental import pallas as pl
from jax.experimental.pallas import tpu as pltpu

F32 = jnp.float32
BF16 = jnp.bfloat16

D_MODEL = 1024
DEPTH = 2
PAST_LEN = 16384
RET_HEADS = 4
RET_DK = D_MODEL // RET_HEADS
RET_DV = 2 * D_MODEL // RET_HEADS
ROPE_BASE = 10000.0
ROPE_HALF = RET_DK // 2
LRU_WIDTH = D_MODEL
LRU_BLOCKS = 4
LRU_BW = LRU_WIDTH // LRU_BLOCKS
LRU_CONV = 4
LRU_C = 8.0
XA_HEADS = 4
XA_HD = D_MODEL // XA_HEADS
FFN_DIM = 3 * D_MODEL
FFN_CONV = 3
EPS = 1e-6

RET_QK = RET_HEADS * RET_DK
RET_V = RET_HEADS * RET_DV
RET_COLS = 2 * RET_QK + 2 * RET_V
RET_GAMMA = tuple(1.0 - 2.0 ** (-5.0 - h) for h in range(RET_HEADS))
RET_LOG_G = tuple(math.log(g) for g in RET_GAMMA)
K_SCALE = RET_DK ** -0.5
XA_SCALE = XA_HD ** -0.5
SQRT_2_OVER_PI = math.sqrt(2.0 / math.pi)

SUBLANES = 8
BF16_SUBLANES = 16
VMEM_LIMIT = 56 << 20

ROW_TILE = 512
WIDE_ROW_TILE = 1024
SCAN_CHUNK = 256
FFN_COLS = 512


def _params(*sem):
    return pltpu.CompilerParams(dimension_semantics=sem, vmem_limit_bytes=VMEM_LIMIT)


def _resident(shape):
    zeros = (0,) * len(shape)
    return pl.BlockSpec(shape, lambda *_: zeros, pipeline_mode=pl.Buffered(1))


def _dot(a, b):
    return jnp.dot(a, b, preferred_element_type=F32)


def _dot_nt(a, b):
    return lax.dot_general(a, b, (((1,), (1,)), ((), ())), preferred_element_type=F32)


def _rms(x, g):
    return x * lax.rsqrt(jnp.mean(x * x, axis=-1, keepdims=True) + EPS) * g


def _gelu(x):
    return x * (0.5 * (1.0 + jnp.tanh(SQRT_2_OVER_PI * (x + 0.044715 * (x * x * x)))))


def _silu(x):
    return x * jax.nn.sigmoid(x)


def _log_sigmoid(x):
    return jnp.minimum(x, 0.0) - jnp.log(1.0 + jnp.exp(-jnp.abs(x)))


def _shift_rows(x, prev, s):
    rolled = pltpu.roll(x, s, 0)
    rows = lax.broadcasted_iota(jnp.int32, (SUBLANES, 1), 0)
    head = jnp.where(rows < s, pltpu.roll(prev, s, 0), rolled[:SUBLANES])
    return jnp.concatenate([head, rolled[SUBLANES:]], axis=0)


def _cast_specs(src, layer, steps, step):
    _, r, n = src.shape
    rows = r // steps
    assert rows * steps == r and rows % BF16_SUBLANES == 0
    return (pl.BlockSpec((None, rows, n), lambda *ids: (layer, step(*ids), 0)),
            pl.BlockSpec((rows, n), lambda *ids: (step(*ids), 0)),
            jax.ShapeDtypeStruct((r, n), BF16))


def _cast_bundle_specs(srcs, steps, step):
    ins, outs, shapes = [], [], []
    for s in srcs:
        layers, r, n = s.shape
        rows = r // steps
        assert rows * steps == r and rows % BF16_SUBLANES == 0
        spec = pl.BlockSpec((layers, rows, n), lambda *ids: (0, step(*ids), 0))
        ins.append(spec)
        outs.append(spec)
        shapes.append(jax.ShapeDtypeStruct(s.shape, BF16))
    return ins, outs, shapes


def _cast_bundle(src_refs, dst_refs):
    for s_ref, d_ref in zip(src_refs, dst_refs, strict=True):
        d_ref[...] = s_ref[...].astype(BF16)


def _ret_proj_kernel(x_ref, g_ref, w_ref, cos_ref, sin_ref, *rest):
    n_cast = len(rest) // 2
    o_ref = rest[n_cast]
    _cast_bundle(rest[:n_cast], rest[n_cast + 1:])
    hn = _rms(x_ref[...], g_ref[...]).astype(BF16)
    cos = cos_ref[...]
    sin = sin_ref[...]
    for n in range(2 * RET_HEADS):
        c0 = n * RET_DK
        acc = _dot(hn, w_ref[:, c0:c0 + RET_DK])
        x1 = acc[:, :ROPE_HALF]
        x2 = acc[:, ROPE_HALF:]
        r1 = x1 * cos - x2 * sin
        r2 = x1 * sin + x2 * cos
        if n >= RET_HEADS:
            r1 = r1 * K_SCALE
            r2 = r2 * K_SCALE
        o_ref[:, c0:c0 + ROPE_HALF] = r1.astype(o_ref.dtype)
        o_ref[:, c0 + ROPE_HALF:c0 + RET_DK] = r2.astype(o_ref.dtype)
    for n in range(2 * RET_V // RET_DV):
        c0 = 2 * RET_QK + n * RET_DV
        o_ref[:, c0:c0 + RET_DV] = _dot(hn, w_ref[:, c0:c0 + RET_DV]).astype(o_ref.dtype)


def _ret_proj(x, g, w, cos, sin, tm, out_dtype, cast=()):
    t = x.shape[0]
    period = cos.shape[0] // tm
    cast_in, cast_out, cast_shape = _cast_bundle_specs(cast, t // tm, lambda m: m)
    return pl.pallas_call(
        _ret_proj_kernel,
        grid=(t // tm,),
        in_specs=[
            pl.BlockSpec((tm, D_MODEL), lambda m: (m, 0)),
            _resident((1, D_MODEL)),
            _resident((D_MODEL, RET_COLS)),
            pl.BlockSpec((tm, ROPE_HALF), lambda m: (m % period, 0)),
            pl.BlockSpec((tm, ROPE_HALF), lambda m: (m % period, 0)),
        ] + cast_in,
        out_specs=[pl.BlockSpec((tm, RET_COLS), lambda m: (m, 0))] + cast_out,
        out_shape=[jax.ShapeDtypeStruct((t, RET_COLS), out_dtype)] + cast_shape,
        compiler_params=_params("arbitrary"),
        name="ret_proj",
    )(x, g, w, cos, sin, *cast)


def _ret_scan_kernel(q_ref, k_ref, v_ref, g_ref, x_ref, w_ref, wup_ref, wdn_ref,
                     xo_ref, s_ref, wup_b_ref, wdn_b_ref, y_scr, intra_scr):
    c = SCAN_CHUNK

    @pl.when((pl.program_id(0) == 0) & (pl.program_id(1) == 0))
    def _():
        row = lax.broadcasted_iota(jnp.int32, (c, c), 0).astype(F32)
        col = lax.broadcasted_iota(jnp.int32, (c, c), 1).astype(F32)
        rel = row - col
        for h in range(RET_HEADS):
            intra_scr[h] = jnp.where(rel >= 0, jnp.exp(RET_LOG_G[h] * jnp.maximum(rel, 0.0)), 0.0)

    @pl.when(pl.program_id(1) == 0)
    def _():
        s_ref[...] = jnp.zeros_like(s_ref)

    idx = lax.broadcasted_iota(jnp.int32, (c, 1), 0).astype(F32)
    for h in range(RET_HEADS):
        lg = RET_LOG_G[h]
        intra = intra_scr[h]
        q_dec = jnp.exp(lg * (idx + 1.0))
        k_dec = jnp.exp(lg * (c - 1.0 - idx))
        chunk_dec = math.exp(lg * c)
        qk_cols = slice(h * RET_DK, (h + 1) * RET_DK)
        v_cols = slice(h * RET_DV, (h + 1) * RET_DV)
        for ci in range(q_ref.shape[0] // c):
            rows = slice(ci * c, (ci + 1) * c)
            qc = q_ref[rows, qk_cols]
            kc = k_ref[rows, qk_cols]
            vc = v_ref[rows, v_cols]
            s = s_ref[h]
            att = _dot_nt(qc, kc) * intra
            o = _dot(att.astype(BF16), vc) + _dot((qc.astype(F32) * q_dec).astype(BF16), s.astype(BF16))
            kd_t = (kc.astype(F32) * k_dec).T.astype(BF16)
            s_ref[h] = s * chunk_dec + _dot(kd_t, vc)
            o = o * lax.rsqrt(jnp.mean(o * o, axis=-1, keepdims=True) + EPS)
            y_scr[rows, v_cols] = (_silu(g_ref[rows, v_cols].astype(F32)) * o).astype(BF16)
    xo_ref[...] = x_ref[...] + _dot(y_scr[...], w_ref[...])
    wup_b_ref[...] = wup_ref[...].astype(BF16)
    wdn_b_ref[...] = wdn_ref[...].astype(BF16)


def _ret_scan(qkvg, x, w_out, ffn_w_up, ffn_w_down, batch, seq, tm):
    t = x.shape[0]
    nj = seq // tm
    step = lambda b, j: b * nj + j
    wup_in, wup_out, wup_shape = _cast_specs(ffn_w_up, 0, batch * nj, step)
    wdn_in, wdn_out, wdn_shape = _cast_specs(ffn_w_down, 0, batch * nj, step)
    return pl.pallas_call(
        _ret_scan_kernel,
        grid=(batch, nj),
        in_specs=[
            pl.BlockSpec((tm, RET_QK), lambda b, j: (b * nj + j, 0)),
            pl.BlockSpec((tm, RET_QK), lambda b, j: (b * nj + j, 1)),
            pl.BlockSpec((tm, RET_V), lambda b, j: (b * nj + j, 1)),
            pl.BlockSpec((tm, RET_V), lambda b, j: (b * nj + j, 2)),
            pl.BlockSpec((tm, D_MODEL), lambda b, j: (b * nj + j, 0)),
            _resident((RET_V, D_MODEL)),
            wup_in,
            wdn_in,
        ],
        out_specs=[
            pl.BlockSpec((tm, D_MODEL), lambda b, j: (b * nj + j, 0)),
            pl.BlockSpec((None, RET_HEADS, RET_DK, RET_DV), lambda b, j: (b, 0, 0, 0)),
            wup_out,
            wdn_out,
        ],
        out_shape=[
            jax.ShapeDtypeStruct((t, D_MODEL), F32),
            jax.ShapeDtypeStruct((batch, RET_HEADS, RET_DK, RET_DV), F32),
            wup_shape,
            wdn_shape,
        ],
        scratch_shapes=[
            pltpu.VMEM((tm, RET_V), BF16),
            pltpu.VMEM((RET_HEADS, SCAN_CHUNK, SCAN_CHUNK), F32),
        ],
        compiler_params=_params("arbitrary", "arbitrary"),
        name="ret_scan",
    )(qkvg, qkvg, qkvg, qkvg, x, w_out, ffn_w_up, ffn_w_down)


def _ret_rows(q_ref, k_ref, v_ref, g_ref, s_ref, so_ref, y_ref):
    n = q_ref.shape[0]
    pad = jnp.zeros((SUBLANES - n, RET_DK), F32)
    v = v_ref[...]
    gate = _silu(g_ref[...])
    for h in range(RET_HEADS):
        qk_cols = slice(h * RET_DK, (h + 1) * RET_DK)
        v_cols = slice(h * RET_DV, (h + 1) * RET_DV)
        q_rows = jnp.concatenate([q_ref[:, qk_cols], pad], axis=0).astype(BF16)
        k_t = jnp.concatenate([k_ref[:, qk_cols], pad], axis=0).T
        for i in range(n):
            s_new = s_ref[i, h] * RET_GAMMA[h] + k_t[:, i:i + 1] * v[i:i + 1, v_cols]
            so_ref[i, h] = s_new
            o = _dot(q_rows, s_new.astype(BF16))[i:i + 1, :]
            o = o * lax.rsqrt(jnp.mean(o * o, axis=-1, keepdims=True) + EPS)
            y_ref[i:i + 1, v_cols] = gate[i:i + 1, v_cols] * o


def _ret_rows_specs(rows, step):
    state = pl.BlockSpec((None, rows, RET_HEADS, RET_DK, RET_DV), lambda *ids: (0, step(*ids), 0, 0, 0))
    ins = [
        pl.BlockSpec((None, rows, RET_QK), lambda *ids: (step(*ids), 0, 0)),
        pl.BlockSpec((None, rows, RET_QK), lambda *ids: (step(*ids), 0, 1)),
        pl.BlockSpec((None, rows, RET_V), lambda *ids: (step(*ids), 0, 1)),
        pl.BlockSpec((None, rows, RET_V), lambda *ids: (step(*ids), 0, 2)),
        state,
    ]
    outs = [state, pl.BlockSpec((None, rows, RET_V), lambda *ids: (step(*ids), 0, 0))]
    return ins, outs


def _norm_matmul_kernel(x_ref, g_ref, w_ref, o_ref):
    o_ref[...] = _dot(_rms(x_ref[...], g_ref[...]).astype(BF16), w_ref[...])


def _norm_matmul(x, g, w):
    return pl.pallas_call(
        _norm_matmul_kernel,
        out_shape=jax.ShapeDtypeStruct((x.shape[0], w.shape[1]), F32),
        compiler_params=pltpu.CompilerParams(vmem_limit_bytes=VMEM_LIMIT),
        name="norm_matmul",
    )(x, g, w)


def _matmul_res_kernel(y_ref, w_ref, x_ref, o_ref):
    o_ref[...] = x_ref[...] + _dot(y_ref[...].astype(BF16), w_ref[...])


def _matmul_res(y, w, x):
    return pl.pallas_call(
        _matmul_res_kernel,
        out_shape=jax.ShapeDtypeStruct(x.shape, F32),
        compiler_params=pltpu.CompilerParams(vmem_limit_bytes=VMEM_LIMIT),
        name="matmul_res",
    )(y, w, x)


def _mem_kv_kernel(m_ref, g_ref, w_ref, wret_ref, k_ref, v_ref, kh_ref, vh_ref, wret_b_ref, w_scr):
    @pl.when(pl.program_id(1) == 0)
    def _():
        w_scr[...] = w_ref[...].astype(BF16)

    wret_b_ref[...] = wret_ref[...].astype(BF16)
    hn = _rms(m_ref[...], g_ref[...]).astype(BF16)
    k = _dot(hn, w_scr[:, :D_MODEL])
    v = _dot(hn, w_scr[:, D_MODEL:])
    k_ref[...] = k
    v_ref[...] = v
    kh_ref[...] = k.reshape(kh_ref.shape)
    vh_ref[...] = v.reshape(vh_ref.shape)


def _mem_kv(mem, g, w, ret_w_in, tm):
    t = mem.shape[0]
    nm = t // tm
    wret_in, wret_out, wret_shape = _cast_specs(ret_w_in, 0, DEPTH * nm, lambda i, m: i * nm + m)
    out = jax.ShapeDtypeStruct((DEPTH, t, D_MODEL), F32)
    out_heads = jax.ShapeDtypeStruct((DEPTH, t, XA_HEADS, XA_HD), F32)
    heads_spec = pl.BlockSpec((None, tm, XA_HEADS, XA_HD), lambda i, m: (i, m, 0, 0))
    return pl.pallas_call(
        _mem_kv_kernel,
        grid=(DEPTH, t // tm),
        in_specs=[
            pl.BlockSpec((tm, D_MODEL), lambda i, m: (m, 0)),
            pl.BlockSpec((None, 1, D_MODEL), lambda i, m: (i, 0, 0)),
            pl.BlockSpec((None, D_MODEL, 2 * D_MODEL), lambda i, m: (i, 0, 0)),
            wret_in,
        ],
        out_specs=[
            pl.BlockSpec((None, tm, D_MODEL), lambda i, m: (i, m, 0)),
            pl.BlockSpec((None, tm, D_MODEL), lambda i, m: (i, m, 0)),
            heads_spec,
            heads_spec,
            wret_out,
        ],
        out_shape=[out, out, out_heads, out_heads, wret_shape],
        scratch_shapes=[pltpu.VMEM((D_MODEL, 2 * D_MODEL), BF16)],
        compiler_params=_params("arbitrary", "arbitrary"),
        name="mem_kv",
    )(mem, g, w, ret_w_in)


def _xattn_rows_specs(rows, mem_len, layer, step):
    row = pl.BlockSpec((rows, XA_HEADS, XA_HD), lambda *ids: (step(*ids), 0, 0))
    mem = pl.BlockSpec((None, rows, mem_len, XA_HEADS, XA_HD), lambda *ids: (layer, step(*ids), 0, 0, 0))
    return [row, mem, mem], row


def _xattn_row(q_ref, mk_ref, mv_ref, o_ref, i):
    mem = mk_ref.shape[1]
    k2 = mk_ref[i].reshape(mem // 2, 2 * XA_HEADS, XA_HD)
    v2 = mv_ref[i].reshape(mem // 2, 2 * XA_HEADS, XA_HD)
    q = q_ref[i]
    s = jnp.sum(k2 * jnp.concatenate([q, q], axis=0)[None], axis=-1, keepdims=True) * XA_SCALE
    m = jnp.max(s, axis=0, keepdims=True)
    e = jnp.exp(s - jnp.maximum(m, pltpu.roll(m, XA_HEADS, 1)))
    l = jnp.sum(e, axis=0, keepdims=True)
    p = e / (l + pltpu.roll(l, XA_HEADS, 1))
    o2 = jnp.sum(p * v2, axis=0)
    o_ref[i] = o2[:XA_HEADS] + o2[XA_HEADS:]


def _xattn_main(x_ref, g_ref, wq_ref, mk_ref, mv_ref, wo_ref, xo_ref, o_scr, per_head=None):
    x = x_ref[...]
    q = _dot(_rms(x, g_ref[...]).astype(BF16), wq_ref[...])
    for h in range(XA_HEADS):
        cols = slice(h * XA_HD, (h + 1) * XA_HD)
        s = _dot_nt(q[:, cols].astype(BF16), mk_ref[:, cols].astype(BF16)) * XA_SCALE
        e = jnp.exp(s - jnp.max(s, axis=-1, keepdims=True))
        p = e / jnp.sum(e, axis=-1, keepdims=True)
        o_scr[:, cols] = _dot(p.astype(BF16), mv_ref[:, cols].astype(BF16)).astype(BF16)
        if per_head is not None:
            per_head(h)
    xo_ref[...] = x + _dot(o_scr[...], wo_ref[...])


def _xattn_kernel(x_ref, g_ref, wq_ref, mk_ref, mv_ref, wo_ref, xo_ref, o_scr):
    _xattn_main(x_ref, g_ref, wq_ref, mk_ref, mv_ref, wo_ref, xo_ref, o_scr)


def _xattn_cast_kernel(x_ref, g_ref, wq_ref, mk_ref, mv_ref, wo_ref, wup_ref, wdn_ref,
                       xo_ref, wup_b_ref, wdn_b_ref, o_scr):
    _xattn_main(x_ref, g_ref, wq_ref, mk_ref, mv_ref, wo_ref, xo_ref, o_scr)
    wup_b_ref[...] = wup_ref[...].astype(BF16)
    wdn_b_ref[...] = wdn_ref[...].astype(BF16)


def _xattn_mem_kernel(x_ref, g_ref, wq_ref, mk_ref, mv_ref, wo_ref, qs_ref, ck_ref, cv_ref,
                      xo_ref, os_ref, o_scr):
    rows = qs_ref.shape[0]
    assert rows <= XA_HEADS

    def per_head(h):
        if h < rows:
            _xattn_row(qs_ref, ck_ref, cv_ref, os_ref, h)

    _xattn_main(x_ref, g_ref, wq_ref, mk_ref, mv_ref, wo_ref, xo_ref, o_scr, per_head)


def _xattn(x, g, wq, mk, mv, wo, layer, batch, seq, tm, mem_step=None, cast_ffn=None):
    t = x.shape[0]
    nj = seq // tm
    mem_len = mk.shape[2]
    step = lambda b, j: b * nj + j
    mem_spec = pl.BlockSpec((None, None, mem_len, D_MODEL), lambda b, j: (layer, b, 0, 0))
    row_spec = pl.BlockSpec((tm, D_MODEL), lambda b, j: (step(b, j), 0))
    in_specs = [row_spec, _resident((1, D_MODEL)), _resident((D_MODEL, D_MODEL)), mem_spec, mem_spec,
                _resident((D_MODEL, D_MODEL))]
    out_specs = [row_spec]
    out_shape = [jax.ShapeDtypeStruct((t, D_MODEL), F32)]
    args = [x, g, wq, mk, mv, wo]
    body = _xattn_kernel
    if mem_step is not None:
        q_s, cache_k, cache_v, cache_layer = mem_step
        rows = q_s.shape[0] // (batch * nj)
        assert rows * batch * nj == q_s.shape[0]
        side_in, side_out = _xattn_rows_specs(rows, cache_k.shape[2], cache_layer, step)
        in_specs += side_in
        out_specs += [side_out]
        out_shape += [jax.ShapeDtypeStruct(q_s.shape, F32)]
        args += [q_s, cache_k, cache_v]
        body = _xattn_mem_kernel
    if cast_ffn is not None:
        assert mem_step is None
        w_up, w_down, cast_layer = cast_ffn
        wup_in, wup_out, wup_shape = _cast_specs(w_up, cast_layer, batch * nj, step)
        wdn_in, wdn_out, wdn_shape = _cast_specs(w_down, cast_layer, batch * nj, step)
        in_specs += [wup_in, wdn_in]
        out_specs += [wup_out, wdn_out]
        out_shape += [wup_shape, wdn_shape]
        args += [w_up, w_down]
        body = _xattn_cast_kernel
    return pl.pallas_call(
        body,
        grid=(batch, nj),
        in_specs=in_specs,
        out_specs=out_specs,
        out_shape=out_shape,
        scratch_shapes=[pltpu.VMEM((tm, D_MODEL), BF16)],
        compiler_params=_params("arbitrary", "arbitrary"),
        name="xattn",
    )(*args)


def _ffn_tail(x, y_scr, wdn_ref, gf_ref, xo_ref, final):
    out = x + _dot(y_scr[...], wdn_ref[...])
    if final:
        out = _rms(out, gf_ref[...])
    xo_ref[...] = out


def _ffn_mem_kernel(x_ref, g_ref, wup_ref, cw_ref, cb_ref, wdn_ref, gf_ref, qs_ref, ck_ref, cv_ref,
                    xo_ref, buf_ref, os_ref, carry_scr, y_scr, *, final):
    def per_chunk(n):
        if n < qs_ref.shape[0]:
            _xattn_row(qs_ref, ck_ref, cv_ref, os_ref, n)

    _ffn_kernel(x_ref, g_ref, wup_ref, cw_ref, cb_ref, wdn_ref, gf_ref, xo_ref, buf_ref, carry_scr, y_scr,
                final=final, per_chunk=per_chunk)


def _ffn_kernel(x_ref, g_ref, wup_ref, cw_ref, cb_ref, wdn_ref, gf_ref, xo_ref, buf_ref,
                carry_scr, y_scr, *, final, per_chunk=None):
    @pl.when(pl.program_id(1) == 0)
    def _():
        carry_scr[...] = jnp.zeros_like(carry_scr)

    x = x_ref[...]
    tm = x.shape[0]
    hn = _rms(x, g_ref[...]).astype(BF16)
    for n in range(FFN_DIM // FFN_COLS):
        cols = slice(n * FFN_COLS, (n + 1) * FFN_COLS)
        gcols = slice(FFN_DIM + n * FFN_COLS, FFN_DIM + (n + 1) * FFN_COLS)
        u = _dot(hn, wup_ref[:, cols])
        gate = _dot(hn, wup_ref[:, gcols])
        prev = carry_scr[:, cols]
        uc = cb_ref[:, cols] + _shift_rows(u, prev, 2) * cw_ref[0:1, cols]
        uc = uc + _shift_rows(u, prev, 1) * cw_ref[1:2, cols]
        uc = uc + u * cw_ref[2:3, cols]
        y_scr[:, cols] = (_gelu(uc) * gate).astype(BF16)
        carry_scr[:, cols] = u[tm - SUBLANES:, :]
        buf_ref[:, cols] = u[tm - (FFN_CONV - 1):, :]
        if per_chunk is not None:
            per_chunk(n)
    _ffn_tail(x, y_scr, wdn_ref, gf_ref, xo_ref, final)


def _ffn(x, g, wup, cw, cb, wdn, gf, batch, seq, tm, final, mem_step=None):
    t = x.shape[0]
    nj = seq // tm
    step = lambda b, j: b * nj + j
    side_in, side_out, side_shape, side_args, body = [], [], [], [], _ffn_kernel
    if mem_step is not None:
        q_s, cache_k, cache_v, cache_layer = mem_step
        rows = q_s.shape[0] // (batch * nj)
        assert rows * batch * nj == q_s.shape[0] and rows <= FFN_DIM // FFN_COLS
        side_in, out = _xattn_rows_specs(rows, cache_k.shape[2], cache_layer, step)
        side_out, side_shape, side_args = [out], [jax.ShapeDtypeStruct(q_s.shape, F32)], [q_s, cache_k, cache_v]
        body = _ffn_mem_kernel
    return pl.pallas_call(
        functools.partial(body, final=final),
        grid=(batch, nj),
        in_specs=[
            pl.BlockSpec((tm, D_MODEL), lambda b, j: (step(b, j), 0)),
            _resident((1, D_MODEL)),
            _resident((D_MODEL, 2 * FFN_DIM)),
            _resident((FFN_CONV, FFN_DIM)),
            _resident((1, FFN_DIM)),
            _resident((FFN_DIM, D_MODEL)),
            _resident((1, D_MODEL)),
        ] + side_in,
        out_specs=[
            pl.BlockSpec((tm, D_MODEL), lambda b, j: (step(b, j), 0)),
            pl.BlockSpec((None, FFN_CONV - 1, FFN_DIM), lambda b, j: (b, 0, 0)),
        ] + side_out,
        out_shape=[
            jax.ShapeDtypeStruct((t, D_MODEL), F32),
            jax.ShapeDtypeStruct((batch, FFN_CONV - 1, FFN_DIM), F32),
        ] + side_shape,
        scratch_shapes=[pltpu.VMEM((SUBLANES, FFN_DIM), F32), pltpu.VMEM((tm, FFN_DIM), BF16)],
        compiler_params=_params("arbitrary", "arbitrary"),
        name="ffn",
    )(x, g, wup, cw, cb, wdn, gf, *side_args)


def _ffn_step_kernel(x_ref, g_ref, wup_ref, cw_ref, cb_ref, wdn_ref, gf_ref, buf_ref, xo_ref, nbuf_ref,
                     y_scr, *, final):
    x = x_ref[...]
    hn = _rms(x, g_ref[...]).astype(BF16)
    for n in range(FFN_DIM // FFN_COLS):
        cols = slice(n * FFN_COLS, (n + 1) * FFN_COLS)
        gcols = slice(FFN_DIM + n * FFN_COLS, FFN_DIM + (n + 1) * FFN_COLS)
        u = _dot(hn, wup_ref[:, cols])
        gate = _dot(hn, wup_ref[:, gcols])
        b0 = buf_ref[:, cols]
        b1 = buf_ref[:, gcols]
        uc = cb_ref[:, cols] + b0 * cw_ref[0:1, cols]
        uc = uc + b1 * cw_ref[1:2, cols]
        uc = uc + u * cw_ref[2:3, cols]
        y_scr[:, cols] = (_gelu(uc) * gate).astype(BF16)
        nbuf_ref[:, cols] = b1
        nbuf_ref[:, gcols] = u
    _ffn_tail(x, y_scr, wdn_ref, gf_ref, xo_ref, final)


def _ffn_step(x, g, wup, cw, cb, wdn, gf, buf, final):
    t = x.shape[0]
    return pl.pallas_call(
        functools.partial(_ffn_step_kernel, final=final),
        out_shape=[
            jax.ShapeDtypeStruct((t, D_MODEL), F32),
            jax.ShapeDtypeStruct(buf.shape, F32),
        ],
        scratch_shapes=[pltpu.VMEM((t, FFN_DIM), BF16)],
        compiler_params=pltpu.CompilerParams(vmem_limit_bytes=VMEM_LIMIT),
        name="ffn_step",
    )(x, g, wup, cw, cb, wdn, gf, buf)


def _lru_gates(xc, wa_ref, ba_ref, wx_ref, bx_ref, lam_ref, a_out, u_out):
    xcb = xc.astype(BF16)
    log_s = _log_sigmoid(lam_ref[...])
    for blk in range(LRU_BLOCKS):
        cols = slice(blk * LRU_BW, (blk + 1) * LRU_BW)
        r = jax.nn.sigmoid(_dot(xcb[:, cols], wa_ref[blk]) + ba_ref[:, cols])
        i = jax.nn.sigmoid(_dot(xcb[:, cols], wx_ref[blk]) + bx_ref[:, cols])
        log_a = LRU_C * r * log_s[:, cols]
        a = jnp.exp(log_a)
        a_out[:, cols] = a
        u_out[:, cols] = jnp.sqrt(-jnp.tanh(log_a) * (a * a + 1.0)) * (i * xc[:, cols])


def _lru_kernel(x_ref, g_ref, win_ref, cw_ref, cb_ref, wa_ref, ba_ref, wx_ref, bx_ref, lam_ref, wout_ref,
                q_ref, k_ref, v_ref, gt_ref, s_ref, xo_ref, h_ref, buf_ref, so_ref, y_ref,
                carry_scr, h_scr, a_scr, u_scr):
    @pl.when(pl.program_id(1) == 0)
    def _():
        carry_scr[...] = jnp.zeros_like(carry_scr)
        h_scr[...] = jnp.zeros_like(h_scr)

    x = x_ref[...]
    tm = x.shape[0]
    hn = _rms(x, g_ref[...]).astype(BF16)
    xb = _dot(hn, win_ref[:, :LRU_WIDTH])
    prev = carry_scr[...]
    xc = cb_ref[...] + _shift_rows(xb, prev, 3) * cw_ref[0:1, :]
    xc = xc + _shift_rows(xb, prev, 2) * cw_ref[1:2, :]
    xc = xc + _shift_rows(xb, prev, 1) * cw_ref[2:3, :]
    xc = xc + xb * cw_ref[3:4, :]
    carry_scr[...] = xb[tm - SUBLANES:, :]
    buf_ref[...] = xb[tm - (LRU_CONV - 1):, :]
    _lru_gates(xc, wa_ref, ba_ref, wx_ref, bx_ref, lam_ref, a_scr, u_scr)

    sub = lax.broadcasted_iota(jnp.int32, (SUBLANES, LRU_WIDTH), 0)

    h = h_scr[...]
    for gi in range(tm // SUBLANES):
        rows = slice(gi * SUBLANES, (gi + 1) * SUBLANES)
        a = a_scr[rows, :]
        u = u_scr[rows, :]
        for s in (1, 2, 4):
            keep = sub >= s
            u = jnp.where(keep, a * pltpu.roll(u, s, 0) + u, u)
            a = jnp.where(keep, a * pltpu.roll(a, s, 0), a)
        hs = a * h + u
        u_scr[rows, :] = hs
        h = hs[SUBLANES - 1:, :]
    h_scr[...] = h
    h_ref[...] = h
    gb = _dot(hn, win_ref[:, LRU_WIDTH:])
    y = (_gelu(gb) * u_scr[...]).astype(BF16)
    xo_ref[...] = x + _dot(y, wout_ref[...])
    _ret_rows(q_ref, k_ref, v_ref, gt_ref, s_ref, so_ref, y_ref)


def _lru_weight_specs():
    return [
        _resident((1, D_MODEL)),
        _resident((D_MODEL, 2 * LRU_WIDTH)),
        _resident((LRU_CONV, LRU_WIDTH)),
        _resident((1, LRU_WIDTH)),
        _resident((LRU_BLOCKS, LRU_BW, LRU_BW)),
        _resident((1, LRU_WIDTH)),
        _resident((LRU_BLOCKS, LRU_BW, LRU_BW)),
        _resident((1, LRU_WIDTH)),
        _resident((1, LRU_WIDTH)),
        _resident((LRU_WIDTH, D_MODEL)),
    ]


def _lru(x, weights, qkvg_s, state, batch, seq, tm):
    t = x.shape[0]
    nj = seq // tm
    steps, rows, _ = qkvg_s.shape
    assert steps == batch * nj and steps * rows == state.shape[1]
    ret_in, ret_out = _ret_rows_specs(rows, lambda b, j: b * nj + j)
    return pl.pallas_call(
        _lru_kernel,
        grid=(batch, nj),
        in_specs=[pl.BlockSpec((tm, D_MODEL), lambda b, j: (b * nj + j, 0))] + _lru_weight_specs() + ret_in,
        out_specs=[
            pl.BlockSpec((tm, D_MODEL), lambda b, j: (b * nj + j, 0)),
            pl.BlockSpec((None, 1, LRU_WIDTH), lambda b, j: (b, 0, 0)),
            pl.BlockSpec((None, LRU_CONV - 1, LRU_WIDTH), lambda b, j: (b, 0, 0)),
        ] + ret_out,
        out_shape=[
            jax.ShapeDtypeStruct((t, D_MODEL), F32),
            jax.ShapeDtypeStruct((batch, 1, LRU_WIDTH), F32),
            jax.ShapeDtypeStruct((batch, LRU_CONV - 1, LRU_WIDTH), F32),
            jax.ShapeDtypeStruct(state.shape, F32),
            jax.ShapeDtypeStruct((steps, rows, RET_V), F32),
        ],
        scratch_shapes=[
            pltpu.VMEM((SUBLANES, LRU_WIDTH), F32),
            pltpu.VMEM((1, LRU_WIDTH), F32),
            pltpu.VMEM((tm, LRU_WIDTH), F32),
            pltpu.VMEM((tm, LRU_WIDTH), F32),
        ],
        compiler_params=_params("arbitrary", "arbitrary"),
        name="lru",
    )(x, *weights, qkvg_s, qkvg_s, qkvg_s, qkvg_s, state)


def _lru_step_kernel(x_ref, g_ref, win_ref, cw_ref, cb_ref, wa_ref, ba_ref, wx_ref, bx_ref, lam_ref, wout_ref,
                     h0_ref, buf_ref, xo_ref, h_ref, nbuf_ref, a_scr, u_scr):
    x = x_ref[...]
    w = LRU_WIDTH
    hn = _rms(x, g_ref[...]).astype(BF16)
    xb = _dot(hn, win_ref[:, :w])
    xc = cb_ref[...] + buf_ref[:, 0:w] * cw_ref[0:1, :]
    xc = xc + buf_ref[:, w:2 * w] * cw_ref[1:2, :]
    xc = xc + buf_ref[:, 2 * w:3 * w] * cw_ref[2:3, :]
    xc = xc + xb * cw_ref[3:4, :]
    nbuf_ref[:, 0:2 * w] = buf_ref[:, w:3 * w]
    nbuf_ref[:, 2 * w:3 * w] = xb
    _lru_gates(xc, wa_ref, ba_ref, wx_ref, bx_ref, lam_ref, a_scr, u_scr)
    hs = a_scr[...] * h0_ref[...] + u_scr[...]
    h_ref[...] = hs
    gb = _dot(hn, win_ref[:, w:])
    xo_ref[...] = x + _dot((_gelu(gb) * hs).astype(BF16), wout_ref[...])


def _lru_step(x, weights, h0, buf):
    t = x.shape[0]
    return pl.pallas_call(
        _lru_step_kernel,
        out_shape=[
            jax.ShapeDtypeStruct((t, D_MODEL), F32),
            jax.ShapeDtypeStruct(h0.shape, F32),
            jax.ShapeDtypeStruct(buf.shape, F32),
        ],
        scratch_shapes=[pltpu.VMEM((t, LRU_WIDTH), F32), pltpu.VMEM((t, LRU_WIDTH), F32)],
        compiler_params=pltpu.CompilerParams(vmem_limit_bytes=VMEM_LIMIT),
        name="lru_step",
    )(x, *weights, h0, buf)


def _rope_tables(positions):
    inv = ROPE_BASE ** (-jnp.arange(ROPE_HALF, dtype=F32) / ROPE_HALF)
    ang = positions[:, None] * inv[None, :]
    return jnp.cos(ang), jnp.sin(ang)


def kernel(x_prompt, x_sample, state_ret, state_lru_h, state_lru_conv, state_ffn_conv, cache_mem_k, cache_mem_v, mem_prompt, norm_mix, norm_xa, norm_mem, norm_ffn, norm_final, ret_w_in, ret_w_out, lru_w_in, lru_conv_w, lru_conv_b, lru_wa, lru_ba, lru_wx, lru_bx, lru_lambda, lru_w_out, xa_w_q, xa_w_kv, xa_w_o, ffn_w_up, ffn_conv_w, ffn_conv_b, ffn_w_down):
    bp, seq, d = x_prompt.shape
    bs = x_sample.shape[0]
    mem_len = mem_prompt.shape[1]
    assert d == D_MODEL and x_sample.shape[1] == 1
    assert seq % WIDE_ROW_TILE == 0 and WIDE_ROW_TILE % ROW_TILE == 0
    assert ROW_TILE % SCAN_CHUNK == 0

    row = lambda v: v.reshape(1, -1)
    gf = row(norm_final)

    steps = bp * (seq // ROW_TILE)
    heads = lambda v: v.reshape(bs, XA_HEADS, XA_HD)
    ffn_args = lambda i, w_up_b, w_down_b: (
        row(norm_ffn[i]), w_up_b, ffn_conv_w[i], row(ffn_conv_b[i]), w_down_b, gf)

    mem_k, mem_v, mem_k_heads, mem_v_heads, ret_w_in_b = _mem_kv(
        mem_prompt.reshape(bp * mem_len, d), norm_mem.reshape(DEPTH, 1, d), xa_w_kv, ret_w_in, ROW_TILE)
    mem_k = mem_k.reshape(DEPTH, bp, mem_len, d)
    mem_v = mem_v.reshape(DEPTH, bp, mem_len, d)
    cos_p, sin_p = _rope_tables(jnp.arange(seq, dtype=F32))
    cos_s, sin_s = _rope_tables(jnp.full((bs,), PAST_LEN, F32))
    xp = x_prompt.reshape(bp * seq, d)
    xs = x_sample.reshape(bs, d)

    later_weights = (ret_w_out, xa_w_q, xa_w_o, lru_w_in, lru_w_out,
                     lru_wa.reshape(-1, LRU_WIDTH, LRU_BW), lru_wx.reshape(-1, LRU_WIDTH, LRU_BW))
    qkvg, ret_w_out_b, xa_w_q_b, xa_w_o_b, lru_w_in_b, lru_w_out_b, lru_wa_b, lru_wx_b = _ret_proj(
        xp, row(norm_mix[0]), ret_w_in_b, cos_p, sin_p, WIDE_ROW_TILE, BF16, cast=later_weights)
    ret_w_out_b = ret_w_out_b[0]
    lru_weights = lambda j: (
        row(norm_mix[1]), lru_w_in_b[j], lru_conv_w[j], row(lru_conv_b[j]),
        lru_wa_b.reshape(lru_wa.shape)[j], row(lru_ba[j]), lru_wx_b.reshape(lru_wx.shape)[j], row(lru_bx[j]),
        row(lru_lambda[j]), lru_w_out_b[j])
    qkvg_s, = _ret_proj(xs, row(norm_mix[0]), ret_w_in_b, cos_s, sin_s, bs, F32)
    qkvg_s = qkvg_s.reshape(steps, bs // steps, RET_COLS)
    xp, ret_p, w_up0, w_down0 = _ret_scan(qkvg, xp, ret_w_out_b, ffn_w_up, ffn_w_down, bp, seq, ROW_TILE)
    xp, w_up1, w_down1 = _xattn(xp, row(norm_xa[0]), xa_w_q_b[0], mem_k, mem_v, xa_w_o_b[0], 0, bp, seq, WIDE_ROW_TILE,
                                cast_ffn=(ffn_w_up, ffn_w_down, 1))
    xp, ffn_p0 = _ffn(xp, *ffn_args(0, w_up0, w_down0), bp, seq, WIDE_ROW_TILE, False)

    xp, lru_h_p, lru_conv_p, ret_s, y_s = _lru(xp, lru_weights(0), qkvg_s, state_ret, bp, seq, ROW_TILE)
    xs = _matmul_res(y_s.reshape(bs, RET_V), ret_w_out_b, xs)
    q_s = heads(_norm_matmul(xs, row(norm_xa[0]), xa_w_q_b[0]))
    xp, o_s = _xattn(xp, row(norm_xa[1]), xa_w_q_b[1], mem_k, mem_v, xa_w_o_b[1], 1, bp, seq, ROW_TILE,
                     mem_step=(q_s, cache_mem_k, cache_mem_v, 0))
    xs = _matmul_res(o_s.reshape(bs, d), xa_w_o_b[0], xs)
    xs, ffn_s0 = _ffn_step(xs, *ffn_args(0, w_up0, w_down0), state_ffn_conv[0].reshape(bs, -1), False)
    xs, lru_h_s, lru_conv_s = _lru_step(xs, lru_weights(0), state_lru_h[0], state_lru_conv[0].reshape(bs, -1))
    q_s = heads(_norm_matmul(xs, row(norm_xa[1]), xa_w_q_b[1]))
    xp, ffn_p1, o_s = _ffn(xp, *ffn_args(1, w_up1, w_down1), bp, seq, ROW_TILE, True,
                           mem_step=(q_s, cache_mem_k, cache_mem_v, 1))
    xs = _matmul_res(o_s.reshape(bs, d), xa_w_o_b[1], xs)
    xs, ffn_s1 = _ffn_step(xs, *ffn_args(1, w_up1, w_down1), state_ffn_conv[1].reshape(bs, -1), True)

    mem_shape = (DEPTH, bp, mem_len, XA_HEADS, XA_HD)
    return (
        xp.reshape(bp, seq, d),
        xs.reshape(bs, 1, d),
        ret_p[None],
        ret_s,
        lru_h_p.reshape(1, bp, LRU_WIDTH),
        lru_h_s[None],
        lru_conv_p[None],
        lru_conv_s.reshape(1, bs, LRU_CONV - 1, LRU_WIDTH),
        jnp.stack([ffn_p0, ffn_p1]),
        jnp.stack([ffn_s0, ffn_s1]).reshape(DEPTH, bs, FFN_CONV - 1, FFN_DIM),
        mem_k_heads.reshape(mem_shape),
        mem_v_heads.reshape(mem_shape),
    )
```
